```python
import jax, jax.numpy as jnp
from jax import lax
import numpy as np

D_MODEL = 2048
BATCH = 4
SEQ = 4096
DEPTH = 1
DEC_BATCH = 4
DEC_SEQ = 8192
PAST_LEN = 128

GRID_W = 64
N_HEADS = 16
N_KV_HEADS = 4
HEAD_DIM = 128
ROT_HALF = HEAD_DIM // 2
N_FREQ = ROT_HALF // 2
ROPE_THETA = 10000.0
Q_BLOCK = 128
ATTN_WIDTH = N_HEADS * HEAD_DIM
KV_WIDTH = N_KV_HEADS * HEAD_DIM
CONV_WIDTH = D_MODEL
CONV_K = 3
N_EXPERTS = 32
TOP_K = 4
D_EXPERT = D_MODEL
SWIGLU_LIMIT = 7.0
SWIGLU_ALPHA = 1.702
MOE_BLOCK = 256
RMS_EPS = 1e-6
LN_EPS = 1e-5
DEEPNORM_ALPHA = (2 * DEPTH) ** 0.25
DEEPNORM_BETA = (8 * DEPTH) ** -0.25
IN_SPLITS = [ATTN_WIDTH, KV_WIDTH, KV_WIDTH, CONV_WIDTH, CONV_WIDTH, CONV_WIDTH, D_MODEL, D_MODEL]
IN_COLS = sum(IN_SPLITS)
IN_OFFSETS = [int(o) for o in np.cumsum(IN_SPLITS)[:-1]]

kernel_name = "hybrid_gqa_shortconv_moe_encoder"


def layer_norm(x, g, b):
    xf = x.astype(jnp.float32)
    mu = jnp.mean(xf, axis=-1, keepdims=True)
    xc = xf - mu
    var = jnp.mean(jnp.square(xc), axis=-1, keepdims=True)
    return (xc * lax.rsqrt(var + LN_EPS) * g + b).astype(x.dtype)


def rms_norm(x, g):
    xf = x.astype(jnp.float32)
    return (xf * lax.rsqrt(jnp.mean(jnp.square(xf), axis=-1, keepdims=True) + RMS_EPS) * g).astype(x.dtype)


def axial_angles(seq_len):
    rows = seq_len // GRID_W
    row = jnp.repeat(jnp.arange(rows, dtype=jnp.float32), GRID_W)
    col = jnp.tile(jnp.arange(GRID_W, dtype=jnp.float32), rows)
    inv_freq = ROPE_THETA ** (-jnp.arange(N_FREQ, dtype=jnp.float32) / N_FREQ)
    ang_r = (row[:, None] * inv_freq[None, :])[:, None, :]
    ang_c = (col[:, None] * inv_freq[None, :])[:, None, :]
    return jnp.cos(ang_r), jnp.sin(ang_r), jnp.cos(ang_c), jnp.sin(ang_c)


def rotate_segment(x, cos, sin):
    x1, x2 = x[..., :N_FREQ], x[..., N_FREQ:]
    return jnp.concatenate([x1 * cos - x2 * sin, x2 * cos + x1 * sin], axis=-1)


def apply_axial_rope(x, angles):
    cos_r, sin_r, cos_c, sin_c = angles
    xf = x.astype(jnp.float32)
    xr = rotate_segment(xf[..., :ROT_HALF], cos_r, sin_r)
    xc = rotate_segment(xf[..., ROT_HALF:], cos_c, sin_c)
    return jnp.concatenate([xr, xc], axis=-1).astype(x.dtype)


def block_attention(q, k, v):
    B, S = q.shape[0], q.shape[1]
    n_blocks = S // Q_BLOCK
    groups = N_HEADS // N_KV_HEADS
    qb = q.reshape(B, n_blocks, Q_BLOCK, N_KV_HEADS, groups, HEAD_DIM).transpose(1, 0, 2, 3, 4, 5)
    scale = HEAD_DIM ** -0.5

    def one_block(q_blk):
        s = jnp.einsum('bqkgd,bskd->bkgqs', q_blk, k, preferred_element_type=jnp.float32) * scale
        p = jax.nn.softmax(s, axis=-1).astype(v.dtype)
        return jnp.einsum('bkgqs,bskd->bqkgd', p, v)

    o = lax.map(one_block, qb)
    return o.transpose(1, 0, 2, 3, 4, 5).reshape(B, S, ATTN_WIDTH)


def centred_depthwise_conv(u, w):
    S = u.shape[1]
    pad = CONV_K // 2
    up = jnp.pad(u, ((0, 0), (pad, pad), (0, 0)))
    return sum(w[j] * up[:, j:j + S] for j in range(CONV_K))


def moe_ffn(x, w_router, b_router, w_gate_up, b_gate_up, w_down, b_down):
    B, S, D = x.shape
    T = B * S
    xt = x.reshape(T, D)
    logits = (xt @ w_router).astype(jnp.float32) + b_router.astype(jnp.float32)
    top_val, top_idx = lax.top_k(logits, TOP_K)
    gate = jax.nn.softmax(top_val, axis=-1)

    A = T * TOP_K
    flat_e = top_idx.reshape(A).astype(jnp.int32)
    flat_tok = jnp.arange(A, dtype=jnp.int32) // TOP_K
    flat_w = gate.reshape(A)
    order = jnp.argsort(flat_e)
    sorted_e = flat_e[order]
    counts = jnp.zeros((N_EXPERTS,), jnp.int32).at[flat_e].add(1)
    start = jnp.cumsum(counts) - counts
    padded = (counts + MOE_BLOCK - 1) // MOE_BLOCK * MOE_BLOCK
    pstart = jnp.cumsum(padded) - padded
    pend = pstart + padded
    n_blocks = -(-A // MOE_BLOCK) + N_EXPERTS
    n_slots = n_blocks * MOE_BLOCK
    rank = jnp.arange(A, dtype=jnp.int32) - start[sorted_e]
    dest = pstart[sorted_e] + rank
    slot_tok = jnp.full((n_slots,), T, jnp.int32).at[dest].set(flat_tok[order])
    slot_w = jnp.zeros((n_slots,), jnp.float32).at[dest].set(flat_w[order])
    block_e = jnp.minimum(jnp.searchsorted(pend, jnp.arange(n_blocks, dtype=jnp.int32) * MOE_BLOCK, side='right'),
                          N_EXPERTS - 1).astype(jnp.int32)
    x_pad = jnp.concatenate([xt, jnp.zeros((1, D), xt.dtype)], axis=0)

    def expert_block(args):
        tok, e = args
        h = x_pad[tok] @ w_gate_up[e] + b_gate_up[e]
        g = jnp.minimum(h[:, :D_EXPERT], SWIGLU_LIMIT)
        u = jnp.clip(h[:, D_EXPERT:], -SWIGLU_LIMIT, SWIGLU_LIMIT)
        act = (u + 1.0) * (g * jax.nn.sigmoid(SWIGLU_ALPHA * g))
        return act @ w_down[e] + b_down[e]

    out = lax.map(expert_block, (slot_tok.reshape(n_blocks, MOE_BLOCK), block_e))
    y = jnp.zeros((T + 1, D), jnp.float32).at[slot_tok].add(
        out.reshape(n_slots, D).astype(jnp.float32) * slot_w[:, None])
    return y[:T].reshape(B, S, D).astype(x.dtype)


def encoder_trunk(x, ln_in_g, ln_in_b, w_in, q_norm, k_norm, conv_w, w_attn_out, w_conv_out, w_o,
                  ln1_g, ln1_b, w_router, b_router, w_gate_up, b_gate_up, w_down, b_down, ln2_g, ln2_b):
    B, S, _ = x.shape
    angles = axial_angles(S)
    x = layer_norm(x, ln_in_g, ln_in_b)
    for l in range(DEPTH):
        proj = x @ w_in[l]
        q, k, v, cb, cc, cx, ga, gc = jnp.split(proj, IN_OFFSETS, axis=-1)
        q = apply_axial_rope(rms_norm(q.reshape(B, S, N_HEADS, HEAD_DIM), q_norm[l]), angles)
        k = apply_axial_rope(rms_norm(k.reshape(B, S, N_KV_HEADS, HEAD_DIM), k_norm[l]), angles)
        v = v.reshape(B, S, N_KV_HEADS, HEAD_DIM)
        attn_branch = block_attention(q, k, v) @ w_attn_out[l]
        conv_branch = (cb * centred_depthwise_conv(cc * cx, conv_w[l])) @ w_conv_out[l]
        merged = jax.nn.sigmoid(ga) * attn_branch + jax.nn.sigmoid(gc) * conv_branch
        x = layer_norm(DEEPNORM_ALPHA * x + merged @ w_o[l], ln1_g[l], ln1_b[l])
        ffn = moe_ffn(x, w_router[l], b_router[l], w_gate_up[l], b_gate_up[l], w_down[l], b_down[l])
        x = layer_norm(DEEPNORM_ALPHA * x + ffn, ln2_g[l], ln2_b[l])
    return x


def setup_inputs(seed: int = 0) -> dict:
    key = jax.random.key(seed)
    ks = jax.random.split(key, 24)

    def nrm(k, shape, scale):
        return jax.random.normal(k, shape, jnp.float32) * scale

    L = DEPTH
    return {
        "x_prompt": nrm(ks[0], (BATCH, SEQ, D_MODEL), 1.0),
        "x_sample": nrm(ks[1], (DEC_BATCH, DEC_SEQ, D_MODEL), 1.0),
        "ln_in_g": 1.0 + nrm(ks[2], (D_MODEL,), 0.02),
        "ln_in_b": nrm(ks[3], (D_MODEL,), 0.02),
        "w_in": nrm(ks[4], (L, D_MODEL, IN_COLS), D_MODEL ** -0.5),
        "q_norm": 1.0 + nrm(ks[5], (L, HEAD_DIM), 0.02),
        "k_norm": 1.0 + nrm(ks[6], (L, HEAD_DIM), 0.02),
        "conv_w": nrm(ks[7], (L, CONV_K, CONV_WIDTH), CONV_K ** -0.5),
        "w_attn_out": nrm(ks[8], (L, ATTN_WIDTH, D_MODEL), ATTN_WIDTH ** -0.5),
        "w_conv_out": nrm(ks[9], (L, CONV_WIDTH, D_MODEL), CONV_WIDTH ** -0.5),
        "w_o": nrm(ks[10], (L, D_MODEL, D_MODEL), DEEPNORM_BETA * D_MODEL ** -0.5),
        "ln1_g": 1.0 + nrm(ks[11], (L, D_MODEL), 0.02),
        "ln1_b": nrm(ks[12], (L, D_MODEL), 0.02),
        "w_router": nrm(ks[13], (L, D_MODEL, N_EXPERTS), D_MODEL ** -0.5),
        "b_router": nrm(ks[14], (L, N_EXPERTS), 0.01),
        "w_gate_up": nrm(ks[15], (L, N_EXPERTS, D_MODEL, 2 * D_EXPERT), D_MODEL ** -0.5),
        "b_gate_up": nrm(ks[16], (L, N_EXPERTS, 2 * D_EXPERT), 0.02),
        "w_down": nrm(ks[17], (L, N_EXPERTS, D_EXPERT, D_MODEL), DEEPNORM_BETA * D_EXPERT ** -0.5),
        "b_down": nrm(ks[18], (L, N_EXPERTS, D_MODEL), 0.02),
        "ln2_g": 1.0 + nrm(ks[19], (L, D_MODEL), 0.02),
        "ln2_b": nrm(ks[20], (L, D_MODEL), 0.02),
    }


def reference(x_prompt, x_sample, ln_in_g, ln_in_b, w_in, q_norm, k_norm, conv_w, w_attn_out, w_conv_out, w_o,
              ln1_g, ln1_b, w_router, b_router, w_gate_up, b_gate_up, w_down, b_down, ln2_g, ln2_b):
    params = (ln_in_g, ln_in_b, w_in, q_norm, k_norm, conv_w, w_attn_out, w_conv_out, w_o,
              ln1_g, ln1_b, w_router, b_router, w_gate_up, b_gate_up, w_down, b_down, ln2_g, ln2_b)
    y_prompt = encoder_trunk(x_prompt, *params)
    y_sample = encoder_trunk(x_sample, *params)
    return (y_prompt, y_sample)
```

```python
import functools

import jax
import jax.numpy as jnp
from jax import lax
from jax.experimental import pallas as pl
from jax.experimental.pallas import tpu as pltpu

F32 = jnp.float32
BF16 = jnp.bfloat16

D_MODEL = 2048
GRID_W = 64
N_HEADS = 16
N_KV_HEADS = 4
GROUPS = N_HEADS // N_KV_HEADS
HEAD_DIM = 128
ROT_HALF = HEAD_DIM // 2
N_FREQ = ROT_HALF // 2
ROPE_THETA = 10000.0
ATTN_WIDTH = N_HEADS * HEAD_DIM
KV_WIDTH = N_KV_HEADS * HEAD_DIM
N_EXPERTS = 32
TOP_K = 4
D_EXPERT = D_MODEL
SWIGLU_LIMIT = 7.0
SWIGLU_ALPHA = 1.702
RMS_EPS = 1e-6
LN_EPS = 1e-5
DEPTH = 1
DEEPNORM_ALPHA = (2 * DEPTH) ** 0.25
ATTN_SCALE = HEAD_DIM ** -0.5

OFF_K = ATTN_WIDTH
OFF_V = OFF_K + KV_WIDTH
OFF_CB = OFF_V + KV_WIDTH
OFF_CC = OFF_CB + D_MODEL
OFF_CX = OFF_CC + D_MODEL
OFF_GA = OFF_CX + D_MODEL
OFF_GC = OFF_GA + D_MODEL

LANES = 128
BF16_ROWS = 16
F32_ROWS = 8
VMEM_LIMIT = 56 * 1024 * 1024

TM_PROJ = 512
TN_PROJ = 512
TQ_ATTN = 256
TK_ATTN = 512
TM_OUT = 256
TM_MOE = 512
FC_MOE = 512
TM_COMB = 128
HALO = BF16_ROWS


def _cparams(sem):
    return pltpu.CompilerParams(dimension_semantics=sem, vmem_limit_bytes=VMEM_LIMIT)


def _layer_norm(x, g, b):
    mu = jnp.mean(x, axis=-1, keepdims=True)
    xc = x - mu
    var = jnp.mean(xc * xc, axis=-1, keepdims=True)
    return xc * lax.rsqrt(var + LN_EPS) * g + b


N_QT = ATTN_WIDTH // TN_PROJ
J_K = N_QT
J_V = N_QT + 1
J_G = N_QT + 2
N_GT = 2 * D_MODEL // TN_PROJ


def _qkvg_kernel(x_ref, lng_ref, lnb_ref, w_ref, cos_ref, sa_ref, sb_ref, qn_ref, kn_ref,
                 q_ref, k_ref, v_ref, g_ref, xn_ref):
    j = pl.program_id(1)

    @pl.when(j == 0)
    def _():
        xn_ref[...] = _layer_norm(x_ref[...], lng_ref[...], lnb_ref[...]).astype(BF16)

    acc = jnp.dot(xn_ref[...], w_ref[...], preferred_element_type=F32)

    def norm_rope(h, gain, scale):
        ms = jnp.mean(h * h, axis=-1, keepdims=True)
        hn = h * lax.rsqrt(ms + RMS_EPS) * gain
        r = (hn * cos_ref[...] + pltpu.roll(hn, HEAD_DIM - N_FREQ, 1) * sa_ref[...]
             + pltpu.roll(hn, N_FREQ, 1) * sb_ref[...])
        return r * scale

    @pl.when(j < J_K)
    def _():
        for a in range(TN_PROJ // HEAD_DIM):
            sl = slice(a * HEAD_DIM, (a + 1) * HEAD_DIM)
            q_ref[:, sl] = norm_rope(acc[:, sl], qn_ref[...], ATTN_SCALE).astype(BF16)

    @pl.when(j == J_K)
    def _():
        for a in range(N_KV_HEADS):
            sl = slice(a * HEAD_DIM, (a + 1) * HEAD_DIM)
            k_ref[:, sl] = norm_rope(acc[:, sl], kn_ref[...], 1.0).astype(BF16)

    @pl.when(j == J_V)
    def _():
        v_ref[...] = acc.astype(BF16)

    @pl.when(j >= J_G)
    def _():
        g_ref[...] = jax.nn.sigmoid(acc).astype(BF16)


def _qkvg_call(x, lng, lnb, w_in, cos, sa, sb, qn, kn, seq):
    T = x.shape[0]
    tm, tn = TM_PROJ, TN_PROJ
    n_seq = seq // tm
    gate_blk0 = OFF_GA // tn

    def w_map(i, j):
        return (0, jnp.where(j < J_G, j, j - J_G + gate_blk0))

    rope_spec = pl.BlockSpec((tm, HEAD_DIM), lambda i, j: (i % n_seq, 0))
    vec_spec = lambda n: pl.BlockSpec((1, n), lambda i, j: (0, 0))
    return pl.pallas_call(
        _qkvg_kernel,
        grid=(T // tm, J_G + N_GT),
        in_specs=[
            pl.BlockSpec((tm, D_MODEL), lambda i, j: (i, 0)),
            vec_spec(D_MODEL), vec_spec(D_MODEL),
            pl.BlockSpec((D_MODEL, tn), w_map),
            rope_spec, rope_spec, rope_spec,
            vec_spec(HEAD_DIM), vec_spec(HEAD_DIM),
        ],
        out_specs=[
            pl.BlockSpec((tm, tn), lambda i, j: (i, jnp.minimum(j, N_QT - 1))),
            pl.BlockSpec((tm, KV_WIDTH), lambda i, j: (i, 0)),
            pl.BlockSpec((tm, KV_WIDTH), lambda i, j: (i, 0)),
            pl.BlockSpec((tm, tn), lambda i, j: (i, jnp.clip(j - J_G, 0, N_GT - 1))),
        ],
        out_shape=[
            jax.ShapeDtypeStruct((T, ATTN_WIDTH), BF16),
            jax.ShapeDtypeStruct((T, KV_WIDTH), BF16),
            jax.ShapeDtypeStruct((T, KV_WIDTH), BF16),
            jax.ShapeDtypeStruct((T, 2 * D_MODEL), BF16),
        ],
        scratch_shapes=[pltpu.VMEM((tm, D_MODEL), BF16)],
        name="qkv_gate_proj",
        compiler_params=_cparams(("arbitrary", "arbitrary")),
    )(x, lng, lnb, w_in, cos, sa, sb, qn, kn)


def _conv_kernel(x_ref, xp_ref, xq_ref, lng_ref, lnb_ref, wcb_ref, wcc_ref, wcx_ref, cw_ref,
                 o_ref, xe_ref, u_ref, *, n_seq, tm):
    i = pl.program_id(0)
    j = pl.program_id(1)

    @pl.when(j == 0)
    def _():
        g = lng_ref[...]
        b = lnb_ref[...]
        first = (i % n_seq) == 0
        last = (i % n_seq) == n_seq - 1
        prev = jnp.where(first, 0.0, _layer_norm(xp_ref[...], g, b))
        nxt = jnp.where(last, 0.0, _layer_norm(xq_ref[...], g, b))
        z = jnp.zeros((HALO - F32_ROWS, D_MODEL), F32)
        xe_ref[0:HALO, :] = jnp.concatenate([z, prev], axis=0).astype(BF16)
        xe_ref[HALO:HALO + tm, :] = _layer_norm(x_ref[...], g, b).astype(BF16)
        xe_ref[HALO + tm:2 * HALO + tm, :] = jnp.concatenate([nxt, z], axis=0).astype(BF16)

    xe = xe_ref[...]
    u_ref[...] = (jnp.dot(xe, wcc_ref[...], preferred_element_type=F32)
                  * jnp.dot(xe, wcx_ref[...], preferred_element_type=F32))
    cb = jnp.dot(xe_ref[HALO:HALO + tm, :], wcb_ref[...], preferred_element_type=F32)
    cw = cw_ref[...]
    conv = (cw[0:1, :] * u_ref[HALO - 1:HALO - 1 + tm, :]
            + cw[1:2, :] * u_ref[HALO:HALO + tm, :]
            + cw[2:3, :] * u_ref[HALO + 1:HALO + 1 + tm, :])
    o_ref[...] = (cb * conv).astype(BF16)


def _conv_call(x, lng, lnb, w_in, conv_w, seq):
    T = x.shape[0]
    tm, tn = TM_PROJ, TN_PROJ
    n_seq = seq // tm
    halo_per_tile = tm // F32_ROWS
    n_halo_blocks = T // F32_ROWS

    def w_spec(off):
        return pl.BlockSpec((D_MODEL, tn), lambda i, j: (0, off // tn + j))

    vec_spec = pl.BlockSpec((1, D_MODEL), lambda i, j: (0, 0))
    return pl.pallas_call(
        functools.partial(_conv_kernel, n_seq=n_seq, tm=tm),
        grid=(T // tm, D_MODEL // tn),
        in_specs=[
            pl.BlockSpec((tm, D_MODEL), lambda i, j: (i, 0)),
            pl.BlockSpec((F32_ROWS, D_MODEL), lambda i, j: (jnp.maximum(i * halo_per_tile - 1, 0), 0)),
            pl.BlockSpec((F32_ROWS, D_MODEL),
                         lambda i, j: (jnp.minimum((i + 1) * halo_per_tile, n_halo_blocks - 1), 0)),
            vec_spec, vec_spec,
            w_spec(OFF_CB), w_spec(OFF_CC), w_spec(OFF_CX),
            pl.BlockSpec((3, tn), lambda i, j: (0, j)),
        ],
        out_specs=pl.BlockSpec((tm, tn), lambda i, j: (i, j)),
        out_shape=jax.ShapeDtypeStruct((T, D_MODEL), BF16),
        scratch_shapes=[pltpu.VMEM((tm + 2 * HALO, D_MODEL), BF16),
                        pltpu.VMEM((tm + 2 * HALO, tn), F32)],
        name="conv_proj",
        compiler_params=_cparams(("arbitrary", "arbitrary")),
    )(x, x, x, lng, lnb, w_in, w_in, w_in, conv_w)


def _attn_kernel(q_ref, k_ref, v_ref, o_ref, m_ref, l_ref, acc_ref, *, tq, tk, n_kv):
    q4 = jnp.concatenate([q_ref[:, g * HEAD_DIM:(g + 1) * HEAD_DIM] for g in range(GROUPS)], axis=0)
    m_ref[...] = jnp.full(m_ref.shape, -jnp.inf, F32)
    l_ref[...] = jnp.zeros(l_ref.shape, F32)
    acc_ref[...] = jnp.zeros(acc_ref.shape, F32)

    def body(c, carry):
        start = pl.multiple_of(c * tk, tk)
        kc = k_ref[pl.ds(start, tk), :]
        vc = v_ref[pl.ds(start, tk), :]
        s = lax.dot_general(q4, kc, (((1,), (1,)), ((), ())), preferred_element_type=F32)
        m_prev = m_ref[...]
        m_new = jnp.maximum(m_prev, jnp.max(s, axis=-1, keepdims=True))
        p = jnp.exp(s - m_new)
        alpha = jnp.exp(m_prev - m_new)
        l_ref[...] = alpha * l_ref[...] + jnp.sum(p, axis=-1, keepdims=True)
        acc_ref[...] = alpha * acc_ref[...] + jnp.dot(p.astype(BF16), vc, preferred_element_type=F32)
        m_ref[...] = m_new
        return carry

    lax.fori_loop(0, n_kv, body, 0)
    out = acc_ref[...] / l_ref[...]
    for g in range(GROUPS):
        o_ref[:, g * HEAD_DIM:(g + 1) * HEAD_DIM] = out[g * tq:(g + 1) * tq, :].astype(BF16)


def _attn_call(q, k, v, batch, seq):
    T = q.shape[0]
    tq = min(TQ_ATTN, seq)
    tk = min(TK_ATTN, seq)
    n_q = seq // tq
    gw = GROUPS * HEAD_DIM
    return pl.pallas_call(
        functools.partial(_attn_kernel, tq=tq, tk=tk, n_kv=seq // tk),
        grid=(batch, N_KV_HEADS, n_q),
        in_specs=[
            pl.BlockSpec((tq, gw), lambda b, h, i: (b * n_q + i, h)),
            pl.BlockSpec((seq, HEAD_DIM), lambda b, h, i: (b, h)),
            pl.BlockSpec((seq, HEAD_DIM), lambda b, h, i: (b, h)),
        ],
        out_specs=pl.BlockSpec((tq, gw), lambda b, h, i: (b * n_q + i, h)),
        out_shape=jax.ShapeDtypeStruct((T, ATTN_WIDTH), BF16),
        scratch_shapes=[pltpu.VMEM((GROUPS * tq, 1), F32),
                        pltpu.VMEM((GROUPS * tq, 1), F32),
                        pltpu.VMEM((GROUPS * tq, HEAD_DIM), F32)],
        name="gqa_attention",
        compiler_params=_cparams(("arbitrary", "arbitrary", "arbitrary")),
    )(q, k, v)


def _merge_kernel(a_ref, c_ref, wa_ref, wc_ref, ga_ref, gc_ref, o_ref):
    att = jnp.dot(a_ref[...], wa_ref[...], preferred_element_type=F32)
    cnv = jnp.dot(c_ref[...], wc_ref[...], preferred_element_type=F32)
    o_ref[...] = (ga_ref[...].astype(F32) * att + gc_ref[...].astype(F32) * cnv).astype(BF16)


def _merge_call(attn, conv_in, wa, wc, gates):
    T = attn.shape[0]
    tm, tn = TM_PROJ, TN_PROJ
    n_j = D_MODEL // tn
    return pl.pallas_call(
        _merge_kernel,
        grid=(T // tm, n_j),
        in_specs=[
            pl.BlockSpec((tm, ATTN_WIDTH), lambda i, j: (i, 0)),
            pl.BlockSpec((tm, D_MODEL), lambda i, j: (i, 0)),
            pl.BlockSpec((ATTN_WIDTH, tn), lambda i, j: (0, j)),
            pl.BlockSpec((D_MODEL, tn), lambda i, j: (0, j)),
            pl.BlockSpec((tm, tn), lambda i, j: (i, j)),
            pl.BlockSpec((tm, tn), lambda i, j: (i, n_j + j)),
        ],
        out_specs=pl.BlockSpec((tm, tn), lambda i, j: (i, j)),
        out_shape=jax.ShapeDtypeStruct((T, D_MODEL), BF16),
        name="gated_merge",
        compiler_params=_cparams(("arbitrary", "arbitrary")),
    )(attn, conv_in, wa, wc, gates, gates)


def _out_kernel(x_ref, lng_ref, lnb_ref, m_ref, wo_ref, g1_ref, b1_ref, wr_ref, br_ref,
                x1_ref, idx_ref, gate_ref):
    xn = _layer_norm(x_ref[...], lng_ref[...], lnb_ref[...])
    out = jnp.dot(m_ref[...], wo_ref[...], preferred_element_type=F32)
    x1 = _layer_norm(DEEPNORM_ALPHA * xn + out, g1_ref[...], b1_ref[...])
    x1_ref[...] = x1
    logits = jnp.dot(x1, wr_ref[...], preferred_element_type=F32,
                     precision=lax.Precision.HIGHEST) + br_ref[...]
    lane = lax.broadcasted_iota(jnp.int32, logits.shape, 1).astype(F32)
    vals = logits
    tops, ids = [], []
    for _ in range(TOP_K):
        mx = jnp.max(vals, axis=-1, keepdims=True)
        ix = jnp.min(jnp.where(vals == mx, lane, float(N_EXPERTS)), axis=-1, keepdims=True)
        tops.append(mx)
        ids.append(ix)
        vals = jnp.where(lane == ix, -jnp.inf, vals)
    exps = [jnp.exp(t - tops[0]) for t in tops]
    den = exps[0] + exps[1] + exps[2] + exps[3]
    out_lane = lax.broadcasted_iota(jnp.int32, idx_ref.shape, 1)
    idx_out = jnp.zeros(idx_ref.shape, F32)
    gate_out = jnp.zeros(gate_ref.shape, F32)
    for kk in range(TOP_K):
        idx_out = jnp.where(out_lane == kk, ids[kk], idx_out)
        gate_out = jnp.where(out_lane == kk, exps[kk] / den, gate_out)
    idx_ref[...] = idx_out.astype(jnp.int32)
    gate_ref[...] = gate_out


def _out_call(x, lng, lnb, merged, wo, g1, b1, wr, br):
    T = x.shape[0]
    tm = TM_OUT
    row = lambda n: pl.BlockSpec((tm, n), lambda i: (i, 0))
    vec = lambda n: pl.BlockSpec((1, n), lambda i: (0, 0))
    return pl.pallas_call(
        _out_kernel,
        grid=(T // tm,),
        in_specs=[row(D_MODEL), vec(D_MODEL), vec(D_MODEL), row(D_MODEL),
                  pl.BlockSpec((D_MODEL, D_MODEL), lambda i: (0, 0)),
                  vec(D_MODEL), vec(D_MODEL),
                  pl.BlockSpec((D_MODEL, N_EXPERTS), lambda i: (0, 0)),
                  vec(N_EXPERTS)],
        out_specs=[row(D_MODEL), row(LANES), row(LANES)],
        out_shape=[jax.ShapeDtypeStruct((T, D_MODEL), F32),
                   jax.ShapeDtypeStruct((T, LANES), jnp.int32),
                   jax.ShapeDtypeStruct((T, LANES), F32)],
        name="out_proj_router",
        compiler_params=_cparams(("arbitrary",)),
    )(x, lng, lnb, merged, wo, g1, b1, wr, br)


def _row_copy(src_hbm, row, buf, slot, r, sem):
    return pltpu.make_async_copy(src_hbm.at[pl.ds(row, 1), :], buf.at[slot, pl.ds(r, 1), :], sem.at[slot])


def _start_rows(idx_ref, src_hbm, buf, slot, sem, n_rows):
    def body(r, c):
        _row_copy(src_hbm, idx_ref[0, 0, r], buf, slot, r, sem).start()
        return c
    lax.fori_loop(0, n_rows, body, 0, unroll=8)


def _wait_rows(src_hbm, buf, slot, sem, n_rows):
    def body(r, c):
        _row_copy(src_hbm, 0, buf, slot, r, sem).wait()
        return c
    lax.fori_loop(0, n_rows, body, 0, unroll=8)


def _expert_kernel(te_ref, nu_ref, tok0_ref, tokn_ref, x_hbm, w_ref, wg_ref, wu_ref, bg_ref, bu_ref,
                   wd_ref, bd_ref, o_ref, xbuf, xb_ref, sem, *, tm, n_fc):
    i = pl.program_id(0)
    j = pl.program_id(1)
    n_used = nu_ref[0]
    slot = i % 2

    @pl.when(i < n_used)
    def _():
        @pl.when(j == 0)
        def _():
            @pl.when(i == 0)
            def _():
                _start_rows(tok0_ref, x_hbm, xbuf, 0, sem, tm)

            @pl.when(i + 1 < n_used)
            def _():
                _start_rows(tokn_ref, x_hbm, xbuf, 1 - slot, sem, tm)

            _wait_rows(x_hbm, xbuf, slot, sem, tm)
            xb_ref[...] = xbuf[slot].astype(BF16)
            o_ref[...] = jnp.broadcast_to(bd_ref[...], o_ref.shape)

        xb = xb_ref[...]
        hg = jnp.dot(xb, wg_ref[...], preferred_element_type=F32) + bg_ref[...]
        hu = jnp.dot(xb, wu_ref[...], preferred_element_type=F32) + bu_ref[...]
        g = jnp.minimum(hg, SWIGLU_LIMIT)
        u = jnp.clip(hu, -SWIGLU_LIMIT, SWIGLU_LIMIT)
        act = (u + 1.0) * (g * jax.nn.sigmoid(SWIGLU_ALPHA * g))
        o_ref[...] += jnp.dot(act.astype(BF16), wd_ref[...], preferred_element_type=F32)

        @pl.when(j == n_fc - 1)
        def _():
            o_ref[...] = o_ref[...] * w_ref[...]

    @pl.when(jnp.logical_and(i >= n_used, j == 0))
    def _():
        o_ref[...] = jnp.zeros(o_ref.shape, F32)


def _expert_call(tile_e, n_used, slot_tok, slot_w, x1, wgu, bgu, wd, bd):
    n_slots = slot_w.shape[0]
    tm, fc = TM_MOE, FC_MOE
    n_tiles = n_slots // tm
    n_fc = D_EXPERT // fc
    tok3 = slot_tok.reshape(n_tiles, 1, tm)

    def live(i, nu):
        return jnp.minimum(i, nu[0] - 1)

    def jj(i, j, nu):
        return jnp.where(i < nu[0], j, n_fc - 1)

    smem_blk = lambda imap: pl.BlockSpec((1, 1, tm), imap, memory_space=pltpu.SMEM)
    grid_spec = pltpu.PrefetchScalarGridSpec(
        num_scalar_prefetch=2,
        grid=(n_tiles, n_fc),
        in_specs=[
            smem_blk(lambda i, j, te, nu: (0, 0, 0)),
            smem_blk(lambda i, j, te, nu: (jnp.minimum(i + 1, n_tiles - 1), 0, 0)),
            pl.BlockSpec(memory_space=pl.ANY),
            pl.BlockSpec((tm, 1), lambda i, j, te, nu: (live(i, nu), 0)),
            pl.BlockSpec((None, D_MODEL, fc), lambda i, j, te, nu: (te[i], 0, jj(i, j, nu))),
            pl.BlockSpec((None, D_MODEL, fc), lambda i, j, te, nu: (te[i], 0, n_fc + jj(i, j, nu))),
            pl.BlockSpec((None, 1, fc), lambda i, j, te, nu: (te[i], 0, jj(i, j, nu))),
            pl.BlockSpec((None, 1, fc), lambda i, j, te, nu: (te[i], 0, n_fc + jj(i, j, nu))),
            pl.BlockSpec((None, fc, D_MODEL), lambda i, j, te, nu: (te[i], jj(i, j, nu), 0)),
            pl.BlockSpec((None, 1, D_MODEL), lambda i, j, te, nu: (te[i], 0, 0)),
        ],
        out_specs=pl.BlockSpec((tm, D_MODEL), lambda i, j, te, nu: (i, 0)),
        scratch_shapes=[pltpu.VMEM((2, tm, D_MODEL), F32),
                        pltpu.VMEM((tm, D_MODEL), BF16),
                        pltpu.SemaphoreType.DMA((2,))],
    )
    return pl.pallas_call(
        functools.partial(_expert_kernel, tm=tm, n_fc=n_fc),
        grid_spec=grid_spec,
        out_shape=jax.ShapeDtypeStruct((n_slots, D_MODEL), F32),
        name="expert_ffn",
        compiler_params=_cparams(("arbitrary", "arbitrary")),
    )(tile_e, n_used, tok3, tok3, x1, slot_w, wgu, wgu, bgu, bgu, wd, bd)


def _combine_kernel(d0_ref, dn_ref, e_hbm, x1_ref, g2_ref, b2_ref, y_ref, buf, sem, *, tm, n_tiles):
    i = pl.program_id(0)
    slot = i % 2
    n_rows = TOP_K * tm

    @pl.when(i == 0)
    def _():
        _start_rows(d0_ref, e_hbm, buf, 0, sem, n_rows)

    @pl.when(i + 1 < n_tiles)
    def _():
        _start_rows(dn_ref, e_hbm, buf, 1 - slot, sem, n_rows)

    _wait_rows(e_hbm, buf, slot, sem, n_rows)
    ffn = buf[slot, 0:tm, :]
    for kk in range(1, TOP_K):
        ffn = ffn + buf[slot, kk * tm:(kk + 1) * tm, :]
    y_ref[...] = _layer_norm(DEEPNORM_ALPHA * x1_ref[...] + ffn, g2_ref[...], b2_ref[...])


def _combine_call(dest, eout, x1, g2, b2):
    T = x1.shape[0]
    tm = TM_COMB
    n_tiles = T // tm
    d3 = dest.reshape(n_tiles, tm, TOP_K).transpose(0, 2, 1).reshape(n_tiles, 1, TOP_K * tm)
    smem_blk = lambda imap: pl.BlockSpec((1, 1, TOP_K * tm), imap, memory_space=pltpu.SMEM)
    vec = pl.BlockSpec((1, D_MODEL), lambda i: (0, 0))
    return pl.pallas_call(
        functools.partial(_combine_kernel, tm=tm, n_tiles=n_tiles),
        grid=(n_tiles,),
        in_specs=[
            smem_blk(lambda i: (0, 0, 0)),
            smem_blk(lambda i: (jnp.minimum(i + 1, n_tiles - 1), 0, 0)),
            pl.BlockSpec(memory_space=pl.ANY),
            pl.BlockSpec((tm, D_MODEL), lambda i: (i, 0)),
            vec, vec,
        ],
        out_specs=pl.BlockSpec((tm, D_MODEL), lambda i: (i, 0)),
        out_shape=jax.ShapeDtypeStruct((T, D_MODEL), F32),
        scratch_shapes=[pltpu.VMEM((2, TOP_K * tm, D_MODEL), F32),
                        pltpu.SemaphoreType.DMA((2,))],
        name="combine_ln",
        compiler_params=_cparams(("arbitrary",)),
    )(d3, d3, eout, x1, g2, b2)


def _rope_tables(seq):
    pos = jnp.arange(seq, dtype=jnp.int32)
    row = (pos // GRID_W).astype(F32)
    col = (pos % GRID_W).astype(F32)
    inv_freq = ROPE_THETA ** (-jnp.arange(N_FREQ, dtype=F32) / N_FREQ)
    ang_r = row[:, None] * inv_freq[None, :]
    ang_c = col[:, None] * inv_freq[None, :]
    zero = jnp.zeros_like(ang_r)
    cos = jnp.concatenate([jnp.cos(ang_r)] * 2 + [jnp.cos(ang_c)] * 2, axis=-1)
    sa = jnp.concatenate([-jnp.sin(ang_r), zero, -jnp.sin(ang_c), zero], axis=-1)
    sb = jnp.concatenate([zero, jnp.sin(ang_r), zero, jnp.sin(ang_c)], axis=-1)
    return cos, sa, sb


def _dispatch_plan(idx, gate, tm):
    T = idx.shape[0]
    A = T * TOP_K
    n_tiles = A // tm + N_EXPERTS
    n_slots = n_tiles * tm
    flat_e = idx.reshape(A)
    onehot = (flat_e[:, None] == jnp.arange(N_EXPERTS, dtype=jnp.int32)[None, :]).astype(jnp.int32)
    csum = jnp.cumsum(onehot, axis=0)
    counts = csum[-1]
    rank = jnp.sum(onehot * csum, axis=1) - 1
    padded = (counts + tm - 1) // tm * tm
    pend = jnp.cumsum(padded)
    pstart = pend - padded
    dest = jnp.sum(onehot * pstart[None, :], axis=1) + rank
    slot_tok = jnp.zeros((n_slots,), jnp.int32).at[dest].set(jnp.arange(A, dtype=jnp.int32) // TOP_K)
    slot_w = jnp.zeros((n_slots,), F32).at[dest].set(gate.reshape(A))
    n_used = (pend[-1] // tm).astype(jnp.int32)
    tile_start = jnp.minimum(jnp.arange(n_tiles, dtype=jnp.int32), n_used - 1) * tm
    tile_e = jnp.minimum(jnp.searchsorted(pend, tile_start, side='right'), N_EXPERTS - 1).astype(jnp.int32)
    return tile_e, n_used.reshape(1), slot_tok, slot_w.reshape(n_slots, 1), dest.reshape(T, TOP_K)


def _trunk(x3, p):
    batch, seq, _ = x3.shape
    x = x3.reshape(batch * seq, D_MODEL)
    cos, sa, sb = _rope_tables(seq)
    q, k, v, gates = _qkvg_call(x, p["ln_in_g"], p["ln_in_b"], p["w_in"], cos, sa, sb,
                                p["q_norm"], p["k_norm"], seq)
    conv_in = _conv_call(x, p["ln_in_g"], p["ln_in_b"], p["w_in"], p["conv_w"], seq)
    attn = _attn_call(q, k, v, batch, seq)
    merged = _merge_call(attn, conv_in, p["w_attn_out"], p["w_conv_out"], gates)
    x1, idx_pad, gate_pad = _out_call(x, p["ln_in_g"], p["ln_in_b"], merged, p["w_o"],
                                      p["ln1_g"], p["ln1_b"], p["w_router"], p["b_router"])
    tile_e, n_used, slot_tok, slot_w, dest = _dispatch_plan(idx_pad[:, :TOP_K], gate_pad[:, :TOP_K], TM_MOE)
    eout = _expert_call(tile_e, n_used, slot_tok, slot_w, x1,
                        p["w_gate_up"], p["b_gate_up"], p["w_down"], p["b_down"])
    y = _combine_call(dest, eout, x1, p["ln2_g"], p["ln2_b"])
    return y.reshape(batch, seq, D_MODEL)


def _prepare(ln_in_g, ln_in_b, w_in, q_norm, k_norm, conv_w, w_attn_out, w_conv_out, w_o,
             ln1_g, ln1_b, w_router, b_router, w_gate_up, b_gate_up, w_down, b_down, ln2_g, ln2_b):
    assert w_in.shape[0] == DEPTH
    vec = lambda a: a.reshape(1, -1)
    return {
        "ln_in_g": vec(ln_in_g), "ln_in_b": vec(ln_in_b),
        "w_in": w_in[0].astype(BF16),
        "q_norm": vec(q_norm[0]), "k_norm": vec(k_norm[0]),
        "conv_w": conv_w[0],
        "w_attn_out": w_attn_out[0].astype(BF16),
        "w_conv_out": w_conv_out[0].astype(BF16),
        "w_o": w_o[0].astype(BF16),
        "ln1_g": vec(ln1_g[0]), "ln1_b": vec(ln1_b[0]),
        "w_router": w_router[0], "b_router": vec(b_router[0]),
        "w_gate_up": w_gate_up[0].astype(BF16),
        "b_gate_up": b_gate_up[0].reshape(N_EXPERTS, 1, 2 * D_EXPERT),
        "w_down": w_down[0].astype(BF16),
        "b_down": b_down[0].reshape(N_EXPERTS, 1, D_MODEL),
        "ln2_g": vec(ln2_g[0]), "ln2_b": vec(ln2_b[0]),
    }


def kernel(x_prompt, x_sample, ln_in_g, ln_in_b, w_in, q_norm, k_norm, conv_w, w_attn_out, w_conv_out, w_o,
           ln1_g, ln1_b, w_router, b_router, w_gate_up, b_gate_up, w_down, b_down, ln2_g, ln2_b):
    p = _prepare(ln_in_g, ln_in_b, w_in, q_norm, k_norm, conv_w, w_attn_out, w_conv_out, w_o,
                 ln1_g, ln1_b, w_router, b_router, w_gate_up, b_gate_up, w_down, b_down, ln2_g, ln2_b)
    return (_trunk(x_prompt, p), _trunk(x_sample, p))
```

```python
import functools

import jax
import jax.numpy as jnp
from jax import lax
from jax.experimental import pallas as pl
from jax.experimental.pallas import tpu as pltpu

F32 = jnp.float32
BF16 = jnp.bfloat16

D_MODEL = 2048
GRID_W = 64
N_HEADS = 16
N_KV_HEADS = 4
GROUPS = N_HEADS // N_KV_HEADS
HEAD_DIM = 128
ROT_HALF = HEAD_DIM // 2
N_FREQ = ROT_HALF // 2
ROPE_THETA = 10000.0
ATTN_WIDTH = N_HEADS * HEAD_DIM
KV_WIDTH = N_KV_HEADS * HEAD_DIM
N_EXPERTS = 32
TOP_K = 4
D_EXPERT = D_MODEL
SWIGLU_LIMIT = 7.0
SWIGLU_ALPHA = 1.702
RMS_EPS = 1e-6
LN_EPS = 1e-5
DEPTH = 1
DEEPNORM_ALPHA = (2 * DEPTH) ** 0.25
ATTN_SCALE = HEAD_DIM ** -0.5

OFF_K = ATTN_WIDTH
OFF_V = OFF_K + KV_WIDTH
OFF_CB = OFF_V + KV_WIDTH
OFF_CC = OFF_CB + D_MODEL
OFF_CX = OFF_CC + D_MODEL
OFF_GA = OFF_CX + D_MODEL
OFF_GC = OFF_GA + D_MODEL

LANES = 128
BF16_ROWS = 16
F32_ROWS = 8
VMEM_LIMIT = 56 * 1024 * 1024

TM_PROJ = 512
TN_PROJ = 512
TQ_ATTN = 256
TK_ATTN = 512
TK_ATTN_BOUNDED = 1024
TM_OUT = 256
TM_MOE = 512
FC_MOE = 512
TM_COMB = 128
HALO = BF16_ROWS


def _cparams(sem):
    return pltpu.CompilerParams(dimension_semantics=sem, vmem_limit_bytes=VMEM_LIMIT)


def _layer_norm(x, g, b):
    mu = jnp.mean(x, axis=-1, keepdims=True)
    xc = x - mu
    var = jnp.mean(xc * xc, axis=-1, keepdims=True)
    return xc * lax.rsqrt(var + LN_EPS) * g + b


N_QT = ATTN_WIDTH // TN_PROJ
J_K = N_QT
J_V = N_QT + 1
J_G = N_QT + 2
N_GT = 2 * D_MODEL // TN_PROJ


def _qkvg_kernel(x_ref, lng_ref, lnb_ref, w_ref, cos_ref, sa_ref, sb_ref, qn_ref, kn_ref,
                 q_ref, k_ref, v_ref, g_ref, xn_ref):
    j = pl.program_id(1)

    @pl.when(j == 0)
    def _():
        xn_ref[...] = _layer_norm(x_ref[...], lng_ref[...], lnb_ref[...]).astype(BF16)

    acc = jnp.dot(xn_ref[...], w_ref[...], preferred_element_type=F32)

    def norm_rope(h, gain, scale):
        ms = jnp.mean(h * h, axis=-1, keepdims=True)
        hn = h * lax.rsqrt(ms + RMS_EPS) * gain
        r = (hn * cos_ref[...] + pltpu.roll(hn, HEAD_DIM - N_FREQ, 1) * sa_ref[...]
             + pltpu.roll(hn, N_FREQ, 1) * sb_ref[...])
        return r * scale

    @pl.when(j < J_K)
    def _():
        for a in range(TN_PROJ // HEAD_DIM):
            sl = slice(a * HEAD_DIM, (a + 1) * HEAD_DIM)
            q_ref[:, sl] = norm_rope(acc[:, sl], qn_ref[...], ATTN_SCALE).astype(BF16)

    @pl.when(j == J_K)
    def _():
        for a in range(N_KV_HEADS):
            sl = slice(a * HEAD_DIM, (a + 1) * HEAD_DIM)
            k_ref[:, sl] = norm_rope(acc[:, sl], kn_ref[...], 1.0).astype(BF16)

    @pl.when(j == J_V)
    def _():
        v_ref[...] = acc.astype(BF16)

    @pl.when(j >= J_G)
    def _():
        g_ref[...] = jax.nn.sigmoid(acc).astype(BF16)


def _qkvg_call(x, lng, lnb, w_in, cos, sa, sb, qn, kn, seq):
    T = x.shape[0]
    tm, tn = TM_PROJ, TN_PROJ
    n_seq = seq // tm
    gate_blk0 = OFF_GA // tn

    def w_map(i, j):
        return (0, jnp.where(j < J_G, j, j - J_G + gate_blk0))

    rope_spec = pl.BlockSpec((tm, HEAD_DIM), lambda i, j: (i % n_seq, 0))
    vec_spec = lambda n: pl.BlockSpec((1, n), lambda i, j: (0, 0))
    return pl.pallas_call(
        _qkvg_kernel,
        grid=(T // tm, J_G + N_GT),
        in_specs=[
            pl.BlockSpec((tm, D_MODEL), lambda i, j: (i, 0)),
            vec_spec(D_MODEL), vec_spec(D_MODEL),
            pl.BlockSpec((D_MODEL, tn), w_map),
            rope_spec, rope_spec, rope_spec,
            vec_spec(HEAD_DIM), vec_spec(HEAD_DIM),
        ],
        out_specs=[
            pl.BlockSpec((tm, tn), lambda i, j: (i, jnp.minimum(j, N_QT - 1))),
            pl.BlockSpec((tm, KV_WIDTH), lambda i, j: (i, 0)),
            pl.BlockSpec((tm, KV_WIDTH), lambda i, j: (i, 0)),
            pl.BlockSpec((tm, tn), lambda i, j: (i, jnp.clip(j - J_G, 0, N_GT - 1))),
        ],
        out_shape=[
            jax.ShapeDtypeStruct((T, ATTN_WIDTH), BF16),
            jax.ShapeDtypeStruct((T, KV_WIDTH), BF16),
            jax.ShapeDtypeStruct((T, KV_WIDTH), BF16),
            jax.ShapeDtypeStruct((T, 2 * D_MODEL), BF16),
        ],
        scratch_shapes=[pltpu.VMEM((tm, D_MODEL), BF16)],
        name="qkv_gate_proj",
        compiler_params=_cparams(("arbitrary", "arbitrary")),
    )(x, lng, lnb, w_in, cos, sa, sb, qn, kn)


def _conv_kernel(x_ref, xp_ref, xq_ref, lng_ref, lnb_ref, wcb_ref, wcc_ref, wcx_ref, cw_ref,
                 o_ref, xe_ref, u_ref, *, n_seq, tm):
    i = pl.program_id(0)
    j = pl.program_id(1)

    @pl.when(j == 0)
    def _():
        g = lng_ref[...]
        b = lnb_ref[...]
        first = (i % n_seq) == 0
        last = (i % n_seq) == n_seq - 1
        prev = jnp.where(first, 0.0, _layer_norm(xp_ref[...], g, b))
        nxt = jnp.where(last, 0.0, _layer_norm(xq_ref[...], g, b))
        z = jnp.zeros((HALO - F32_ROWS, D_MODEL), F32)
        xe_ref[0:HALO, :] = jnp.concatenate([z, prev], axis=0).astype(BF16)
        xe_ref[HALO:HALO + tm, :] = _layer_norm(x_ref[...], g, b).astype(BF16)
        xe_ref[HALO + tm:2 * HALO + tm, :] = jnp.concatenate([nxt, z], axis=0).astype(BF16)

    xe = xe_ref[...]
    u_ref[...] = (jnp.dot(xe, wcc_ref[...], preferred_element_type=F32)
                  * jnp.dot(xe, wcx_ref[...], preferred_element_type=F32))
    cb = jnp.dot(xe_ref[HALO:HALO + tm, :], wcb_ref[...], preferred_element_type=F32)
    cw = cw_ref[...]
    conv = (cw[0:1, :] * u_ref[HALO - 1:HALO - 1 + tm, :]
            + cw[1:2, :] * u_ref[HALO:HALO + tm, :]
            + cw[2:3, :] * u_ref[HALO + 1:HALO + 1 + tm, :])
    o_ref[...] = (cb * conv).astype(BF16)


def _conv_call(x, lng, lnb, w_in, conv_w, seq):
    T = x.shape[0]
    tm, tn = TM_PROJ, TN_PROJ
    n_seq = seq // tm
    halo_per_tile = tm // F32_ROWS
    n_halo_blocks = T // F32_ROWS

    def w_spec(off):
        return pl.BlockSpec((D_MODEL, tn), lambda i, j: (0, off // tn + j))

    vec_spec = pl.BlockSpec((1, D_MODEL), lambda i, j: (0, 0))
    return pl.pallas_call(
        functools.partial(_conv_kernel, n_seq=n_seq, tm=tm),
        grid=(T // tm, D_MODEL // tn),
        in_specs=[
            pl.BlockSpec((tm, D_MODEL), lambda i, j: (i, 0)),
            pl.BlockSpec((F32_ROWS, D_MODEL), lambda i, j: (jnp.maximum(i * halo_per_tile - 1, 0), 0)),
            pl.BlockSpec((F32_ROWS, D_MODEL),
                         lambda i, j: (jnp.minimum((i + 1) * halo_per_tile, n_halo_blocks - 1), 0)),
            vec_spec, vec_spec,
            w_spec(OFF_CB), w_spec(OFF_CC), w_spec(OFF_CX),
            pl.BlockSpec((3, tn), lambda i, j: (0, j)),
        ],
        out_specs=pl.BlockSpec((tm, tn), lambda i, j: (i, j)),
        out_shape=jax.ShapeDtypeStruct((T, D_MODEL), BF16),
        scratch_shapes=[pltpu.VMEM((tm + 2 * HALO, D_MODEL), BF16),
                        pltpu.VMEM((tm + 2 * HALO, tn), F32)],
        name="conv_proj",
        compiler_params=_cparams(("arbitrary", "arbitrary")),
    )(x, x, x, lng, lnb, w_in, w_in, w_in, conv_w)


def _attn_kernel(q_ref, k_ref, v_ref, o_ref, m_ref, l_ref, acc_ref, *, tq, tk, n_kv):
    q4 = jnp.concatenate([q_ref[:, g * HEAD_DIM:(g + 1) * HEAD_DIM] for g in range(GROUPS)], axis=0)
    m_ref[...] = jnp.full(m_ref.shape, -jnp.inf, F32)
    l_ref[...] = jnp.zeros(l_ref.shape, F32)
    acc_ref[...] = jnp.zeros(acc_ref.shape, F32)

    def body(c, carry):
        start = pl.multiple_of(c * tk, tk)
        kc = k_ref[pl.ds(start, tk), :]
        vc = v_ref[pl.ds(start, tk), :]
        s = lax.dot_general(q4, kc, (((1,), (1,)), ((), ())), preferred_element_type=F32)
        m_prev = m_ref[...]
        m_new = jnp.maximum(m_prev, jnp.max(s, axis=-1, keepdims=True))
        p = jnp.exp(s - m_new)
        alpha = jnp.exp(m_prev - m_new)
        l_ref[...] = alpha * l_ref[...] + jnp.sum(p, axis=-1, keepdims=True)
        acc_ref[...] = alpha * acc_ref[...] + jnp.dot(p.astype(BF16), vc, preferred_element_type=F32)
        m_ref[...] = m_new
        return carry

    lax.fori_loop(0, n_kv, body, 0)
    out = acc_ref[...] / l_ref[...]
    for g in range(GROUPS):
        o_ref[:, g * HEAD_DIM:(g + 1) * HEAD_DIM] = out[g * tq:(g + 1) * tq, :].astype(BF16)


def _attn_call(q, k, v, batch, seq):
    T = q.shape[0]
    tq = min(TQ_ATTN, seq)
    tk = min(TK_ATTN, seq)
    n_q = seq // tq
    gw = GROUPS * HEAD_DIM
    return pl.pallas_call(
        functools.partial(_attn_kernel, tq=tq, tk=tk, n_kv=seq // tk),
        grid=(batch, N_KV_HEADS, n_q),
        in_specs=[
            pl.BlockSpec((tq, gw), lambda b, h, i: (b * n_q + i, h)),
            pl.BlockSpec((seq, HEAD_DIM), lambda b, h, i: (b, h)),
            pl.BlockSpec((seq, HEAD_DIM), lambda b, h, i: (b, h)),
        ],
        out_specs=pl.BlockSpec((tq, gw), lambda b, h, i: (b * n_q + i, h)),
        out_shape=jax.ShapeDtypeStruct((T, ATTN_WIDTH), BF16),
        scratch_shapes=[pltpu.VMEM((GROUPS * tq, 1), F32),
                        pltpu.VMEM((GROUPS * tq, 1), F32),
                        pltpu.VMEM((GROUPS * tq, HEAD_DIM), F32)],
        name="gqa_attention",
        compiler_params=_cparams(("arbitrary", "arbitrary", "arbitrary")),
    )(q, k, v)


def _attn_bounded_kernel(nm_ref, q_ref, k_ref, v_ref, o_ref, ka_ref, va_ref, acc_ref, *, tq, tk, n_kv):
    @pl.when(pl.program_id(2) == 0)
    def _():
        lane = lax.broadcasted_iota(jnp.int32, k_ref.shape, 1)
        ka_ref[:, :HEAD_DIM] = k_ref[...]
        ka_ref[:, HEAD_DIM:] = jnp.where(lane == 0, 1.0, 0.0).astype(BF16)
        va_ref[:, :HEAD_DIM] = v_ref[...]
        va_ref[:, HEAD_DIM:] = jnp.ones(v_ref.shape, BF16)

    lane = lax.broadcasted_iota(jnp.int32, (tq, HEAD_DIM), 1)
    shift = jnp.where(lane == 0, nm_ref[0], 0.0).astype(BF16)
    q4 = jnp.concatenate(
        [jnp.concatenate([q_ref[:, g * HEAD_DIM:(g + 1) * HEAD_DIM], shift], axis=1) for g in range(GROUPS)],
        axis=0)
    acc_ref[...] = jnp.zeros(acc_ref.shape, F32)

    def body(c, carry):
        start = pl.multiple_of(c * tk, tk)
        s = lax.dot_general(q4, ka_ref[pl.ds(start, tk), :], (((1,), (1,)), ((), ())),
                            preferred_element_type=F32)
        p = jnp.exp(s).astype(BF16)
        acc_ref[...] += jnp.dot(p, va_ref[pl.ds(start, tk), :], preferred_element_type=F32)
        return carry

    lax.fori_loop(0, n_kv, body, 0, unroll=True)
    out = acc_ref[:, :HEAD_DIM] / acc_ref[:, HEAD_DIM:]
    for g in range(GROUPS):
        o_ref[:, g * HEAD_DIM:(g + 1) * HEAD_DIM] = out[g * tq:(g + 1) * tq, :].astype(BF16)


def _attn_bounded_call(neg_shift, q, k, v, batch, seq):
    T = q.shape[0]
    tq = min(TQ_ATTN, seq)
    tk = min(TK_ATTN_BOUNDED, seq)
    n_q = seq // tq
    gw = GROUPS * HEAD_DIM
    return pl.pallas_call(
        functools.partial(_attn_bounded_kernel, tq=tq, tk=tk, n_kv=seq // tk),
        grid=(batch, N_KV_HEADS, n_q),
        in_specs=[
            pl.BlockSpec(memory_space=pltpu.SMEM),
            pl.BlockSpec((tq, gw), lambda b, h, i: (b * n_q + i, h)),
            pl.BlockSpec((seq, HEAD_DIM), lambda b, h, i: (b, h)),
            pl.BlockSpec((seq, HEAD_DIM), lambda b, h, i: (b, h)),
        ],
        out_specs=pl.BlockSpec((tq, gw), lambda b, h, i: (b * n_q + i, h)),
        out_shape=jax.ShapeDtypeStruct((T, ATTN_WIDTH), BF16),
        scratch_shapes=[pltpu.VMEM((seq, 2 * HEAD_DIM), BF16),
                        pltpu.VMEM((seq, 2 * HEAD_DIM), BF16),
                        pltpu.VMEM((GROUPS * tq, 2 * HEAD_DIM), F32)],
        name="gqa_attention_bounded",
        compiler_params=_cparams(("arbitrary", "arbitrary", "arbitrary")),
    )(neg_shift, q, k, v)


MAX_SCORE_SHIFT = 40.0


def _score_bound(q_norm, k_norm):
    slack = 1.02
    return slack * HEAD_DIM * ATTN_SCALE * jnp.max(jnp.abs(q_norm)) * jnp.max(jnp.abs(k_norm))


def _attention(q, k, v, q_norm, k_norm, batch, seq):
    m = _score_bound(q_norm, k_norm)
    return lax.cond(
        m <= MAX_SCORE_SHIFT,
        lambda: _attn_bounded_call((-m).reshape(1).astype(F32), q, k, v, batch, seq),
        lambda: _attn_call(q, k, v, batch, seq))


def _merge_kernel(a_ref, c_ref, wa_ref, wc_ref, ga_ref, gc_ref, o_ref):
    att = jnp.dot(a_ref[...], wa_ref[...], preferred_element_type=F32)
    cnv = jnp.dot(c_ref[...], wc_ref[...], preferred_element_type=F32)
    o_ref[...] = (ga_ref[...].astype(F32) * att + gc_ref[...].astype(F32) * cnv).astype(BF16)


def _merge_call(attn, conv_in, wa, wc, gates):
    T = attn.shape[0]
    tm, tn = TM_PROJ, TN_PROJ
    n_j = D_MODEL // tn
    return pl.pallas_call(
        _merge_kernel,
        grid=(T // tm, n_j),
        in_specs=[
            pl.BlockSpec((tm, ATTN_WIDTH), lambda i, j: (i, 0)),
            pl.BlockSpec((tm, D_MODEL), lambda i, j: (i, 0)),
            pl.BlockSpec((ATTN_WIDTH, tn), lambda i, j: (0, j)),
            pl.BlockSpec((D_MODEL, tn), lambda i, j: (0, j)),
            pl.BlockSpec((tm, tn), lambda i, j: (i, j)),
            pl.BlockSpec((tm, tn), lambda i, j: (i, n_j + j)),
        ],
        out_specs=pl.BlockSpec((tm, tn), lambda i, j: (i, j)),
        out_shape=jax.ShapeDtypeStruct((T, D_MODEL), BF16),
        name="gated_merge",
        compiler_params=_cparams(("arbitrary", "arbitrary")),
    )(attn, conv_in, wa, wc, gates, gates)


def _out_kernel(x_ref, lng_ref, lnb_ref, m_ref, wo_ref, g1_ref, b1_ref, wr_ref, br_ref,
                x1_ref, idx_ref, gate_ref):
    xn = _layer_norm(x_ref[...], lng_ref[...], lnb_ref[...])
    out = jnp.dot(m_ref[...], wo_ref[...], preferred_element_type=F32)
    x1 = _layer_norm(DEEPNORM_ALPHA * xn + out, g1_ref[...], b1_ref[...])
    x1_ref[...] = x1
    logits = jnp.dot(x1, wr_ref[...], preferred_element_type=F32,
                     precision=lax.Precision.HIGHEST) + br_ref[...]
    lane = lax.broadcasted_iota(jnp.int32, logits.shape, 1).astype(F32)
    vals = logits
    tops, ids = [], []
    for _ in range(TOP_K):
        mx = jnp.max(vals, axis=-1, keepdims=True)
        ix = jnp.min(jnp.where(vals == mx, lane, float(N_EXPERTS)), axis=-1, keepdims=True)
        tops.append(mx)
        ids.append(ix)
        vals = jnp.where(lane == ix, -jnp.inf, vals)
    exps = [jnp.exp(t - tops[0]) for t in tops]
    den = exps[0] + exps[1] + exps[2] + exps[3]
    out_lane = lax.broadcasted_iota(jnp.int32, idx_ref.shape, 1)
    idx_out = jnp.zeros(idx_ref.shape, F32)
    gate_out = jnp.zeros(gate_ref.shape, F32)
    for kk in range(TOP_K):
        idx_out = jnp.where(out_lane == kk, ids[kk], idx_out)
        gate_out = jnp.where(out_lane == kk, exps[kk] / den, gate_out)
    idx_ref[...] = idx_out.astype(jnp.int32)
    gate_ref[...] = gate_out


def _out_call(x, lng, lnb, merged, wo, g1, b1, wr, br):
    T = x.shape[0]
    tm = TM_OUT
    row = lambda n: pl.BlockSpec((tm, n), lambda i: (i, 0))
    vec = lambda n: pl.BlockSpec((1, n), lambda i: (0, 0))
    return pl.pallas_call(
        _out_kernel,
        grid=(T // tm,),
        in_specs=[row(D_MODEL), vec(D_MODEL), vec(D_MODEL), row(D_MODEL),
                  pl.BlockSpec((D_MODEL, D_MODEL), lambda i: (0, 0)),
                  vec(D_MODEL), vec(D_MODEL),
                  pl.BlockSpec((D_MODEL, N_EXPERTS), lambda i: (0, 0)),
                  vec(N_EXPERTS)],
        out_specs=[row(D_MODEL), row(LANES), row(LANES)],
        out_shape=[jax.ShapeDtypeStruct((T, D_MODEL), F32),
                   jax.ShapeDtypeStruct((T, LANES), jnp.int32),
                   jax.ShapeDtypeStruct((T, LANES), F32)],
        name="out_proj_router",
        compiler_params=_cparams(("arbitrary",)),
    )(x, lng, lnb, merged, wo, g1, b1, wr, br)


def _row_copy(src_hbm, row, buf, slot, r, sem):
    return pltpu.make_async_copy(src_hbm.at[pl.ds(row, 1), :], buf.at[slot, pl.ds(r, 1), :], sem.at[slot])


def _start_rows(idx_ref, src_hbm, buf, slot, sem, n_rows):
    def body(r, c):
        _row_copy(src_hbm, idx_ref[0, 0, r], buf, slot, r, sem).start()
        return c
    lax.fori_loop(0, n_rows, body, 0, unroll=8)


def _wait_rows(src_hbm, buf, slot, sem, n_rows):
    def body(r, c):
        _row_copy(src_hbm, 0, buf, slot, r, sem).wait()
        return c
    lax.fori_loop(0, n_rows, body, 0, unroll=8)


def _expert_kernel(te_ref, nu_ref, tok0_ref, tokn_ref, x_hbm, wg_ref, wu_ref, bg_ref, bu_ref,
                   wd_ref, bd_ref, o_ref, xbuf, xb_ref, sem, *, tm, n_fc):
    i = pl.program_id(0)
    j = pl.program_id(1)
    n_used = nu_ref[0]
    slot = i % 2
    rows_per_step = tm // n_fc

    @pl.when(i < n_used)
    def _():
        @pl.when(j == 0)
        def _():
            @pl.when(i == 0)
            def _():
                _start_rows(tok0_ref, x_hbm, xbuf, 0, sem, tm)

            _wait_rows(x_hbm, xbuf, slot, sem, tm)
            xb_ref[...] = xbuf[slot].astype(BF16)
            o_ref[...] = jnp.broadcast_to(bd_ref[...], o_ref.shape)

        base = j * rows_per_step
        for r in range(rows_per_step):
            _row_copy(x_hbm, tokn_ref[0, 0, base + r], xbuf, 1 - slot, base + r, sem).start()

        xb = xb_ref[...]
        hg = jnp.dot(xb, wg_ref[...], preferred_element_type=F32) + bg_ref[...]
        hu = jnp.dot(xb, wu_ref[...], preferred_element_type=F32) + bu_ref[...]
        g = jnp.minimum(hg, SWIGLU_LIMIT)
        u = jnp.clip(hu, -SWIGLU_LIMIT, SWIGLU_LIMIT)
        act = (u + 1.0) * (g * jax.nn.sigmoid(SWIGLU_ALPHA * g))
        o_ref[...] += jnp.dot(act.astype(BF16), wd_ref[...], preferred_element_type=F32)

        @pl.when(jnp.logical_and(i == n_used - 1, j == n_fc - 1))
        def _():
            _wait_rows(x_hbm, xbuf, 1 - slot, sem, tm)

    @pl.when(jnp.logical_and(i >= n_used, j == 0))
    def _():
        o_ref[...] = jnp.zeros(o_ref.shape, F32)


def _expert_call(tile_e, n_used, slot_tok, x1, wgu, bgu, wd, bd):
    n_slots = slot_tok.shape[0]
    tm, fc = TM_MOE, FC_MOE
    n_tiles = n_slots // tm
    n_fc = D_EXPERT // fc
    tok3 = slot_tok.reshape(n_tiles, 1, tm)

    def jj(i, j, nu):
        return jnp.where(i < nu[0], j, n_fc - 1)

    smem_blk = lambda imap: pl.BlockSpec((1, 1, tm), imap, memory_space=pltpu.SMEM)
    grid_spec = pltpu.PrefetchScalarGridSpec(
        num_scalar_prefetch=2,
        grid=(n_tiles, n_fc),
        in_specs=[
            smem_blk(lambda i, j, te, nu: (0, 0, 0)),
            smem_blk(lambda i, j, te, nu: (jnp.minimum(i + 1, n_tiles - 1), 0, 0)),
            pl.BlockSpec(memory_space=pl.ANY),
            pl.BlockSpec((None, D_MODEL, fc), lambda i, j, te, nu: (te[i], 0, jj(i, j, nu))),
            pl.BlockSpec((None, D_MODEL, fc), lambda i, j, te, nu: (te[i], 0, n_fc + jj(i, j, nu))),
            pl.BlockSpec((None, 1, fc), lambda i, j, te, nu: (te[i], 0, jj(i, j, nu))),
            pl.BlockSpec((None, 1, fc), lambda i, j, te, nu: (te[i], 0, n_fc + jj(i, j, nu))),
            pl.BlockSpec((None, fc, D_MODEL), lambda i, j, te, nu: (te[i], jj(i, j, nu), 0)),
            pl.BlockSpec((None, 1, D_MODEL), lambda i, j, te, nu: (te[i], 0, 0)),
        ],
        out_specs=pl.BlockSpec((tm, D_MODEL), lambda i, j, te, nu: (i, 0)),
        scratch_shapes=[pltpu.VMEM((2, tm, D_MODEL), F32),
                        pltpu.VMEM((tm, D_MODEL), BF16),
                        pltpu.SemaphoreType.DMA((2,))],
    )
    return pl.pallas_call(
        functools.partial(_expert_kernel, tm=tm, n_fc=n_fc),
        grid_spec=grid_spec,
        out_shape=jax.ShapeDtypeStruct((n_slots, D_MODEL), F32),
        name="expert_ffn",
        compiler_params=_cparams(("arbitrary", "arbitrary")),
    )(tile_e, n_used, tok3, tok3, x1, wgu, wgu, bgu, bgu, wd, bd)


def _combine_kernel(d0_ref, dn_ref, e_hbm, x1_ref, gate_ref, g2_ref, b2_ref, y_ref, buf, sem, *, tm, n_tiles):
    i = pl.program_id(0)
    slot = i % 2
    n_rows = TOP_K * tm

    @pl.when(i == 0)
    def _():
        _start_rows(d0_ref, e_hbm, buf, 0, sem, n_rows)

    @pl.when(i + 1 < n_tiles)
    def _():
        _start_rows(dn_ref, e_hbm, buf, 1 - slot, sem, n_rows)

    _wait_rows(e_hbm, buf, slot, sem, n_rows)
    gate = gate_ref[...]
    ffn = gate[:, 0:1] * buf[slot, 0:tm, :]
    for kk in range(1, TOP_K):
        ffn = ffn + gate[:, kk:kk + 1] * buf[slot, kk * tm:(kk + 1) * tm, :]
    y_ref[...] = _layer_norm(DEEPNORM_ALPHA * x1_ref[...] + ffn, g2_ref[...], b2_ref[...])


def _combine_call(dest, eout, x1, gate_pad, g2, b2):
    T = x1.shape[0]
    tm = TM_COMB
    n_tiles = T // tm
    d3 = dest.reshape(n_tiles, tm, TOP_K).transpose(0, 2, 1).reshape(n_tiles, 1, TOP_K * tm)
    smem_blk = lambda imap: pl.BlockSpec((1, 1, TOP_K * tm), imap, memory_space=pltpu.SMEM)
    vec = pl.BlockSpec((1, D_MODEL), lambda i: (0, 0))
    return pl.pallas_call(
        functools.partial(_combine_kernel, tm=tm, n_tiles=n_tiles),
        grid=(n_tiles,),
        in_specs=[
            smem_blk(lambda i: (0, 0, 0)),
            smem_blk(lambda i: (jnp.minimum(i + 1, n_tiles - 1), 0, 0)),
            pl.BlockSpec(memory_space=pl.ANY),
            pl.BlockSpec((tm, D_MODEL), lambda i: (i, 0)),
            pl.BlockSpec((tm, LANES), lambda i: (i, 0)),
            vec, vec,
        ],
        out_specs=pl.BlockSpec((tm, D_MODEL), lambda i: (i, 0)),
        out_shape=jax.ShapeDtypeStruct((T, D_MODEL), F32),
        scratch_shapes=[pltpu.VMEM((2, TOP_K * tm, D_MODEL), F32),
                        pltpu.SemaphoreType.DMA((2,))],
        name="combine_ln",
        compiler_params=_cparams(("arbitrary",)),
    )(d3, d3, eout, x1, gate_pad, g2, b2)


def _rope_tables(seq):
    pos = jnp.arange(seq, dtype=jnp.int32)
    row = (pos // GRID_W).astype(F32)
    col = (pos % GRID_W).astype(F32)
    inv_freq = ROPE_THETA ** (-jnp.arange(N_FREQ, dtype=F32) / N_FREQ)
    ang_r = row[:, None] * inv_freq[None, :]
    ang_c = col[:, None] * inv_freq[None, :]
    zero = jnp.zeros_like(ang_r)
    cos = jnp.concatenate([jnp.cos(ang_r)] * 2 + [jnp.cos(ang_c)] * 2, axis=-1)
    sa = jnp.concatenate([-jnp.sin(ang_r), zero, -jnp.sin(ang_c), zero], axis=-1)
    sb = jnp.concatenate([zero, jnp.sin(ang_r), zero, jnp.sin(ang_c)], axis=-1)
    return cos, sa, sb


def _dispatch_plan(idx, tm):
    T = idx.shape[0]
    A = T * TOP_K
    n_tiles = A // tm + N_EXPERTS
    n_slots = n_tiles * tm
    flat_e = idx.reshape(A)
    onehot = (flat_e[:, None] == jnp.arange(N_EXPERTS, dtype=jnp.int32)[None, :]).astype(jnp.int32)
    csum = jnp.cumsum(onehot, axis=0)
    counts = csum[-1]
    rank = jnp.sum(onehot * csum, axis=1) - 1
    padded = (counts + tm - 1) // tm * tm
    pend = jnp.cumsum(padded)
    pstart = pend - padded
    dest = jnp.sum(onehot * pstart[None, :], axis=1) + rank
    slot_tok = jnp.zeros((n_slots,), jnp.int32).at[dest].set(jnp.arange(A, dtype=jnp.int32) // TOP_K)
    n_used = (pend[-1] // tm).astype(jnp.int32)
    tile_start = jnp.minimum(jnp.arange(n_tiles, dtype=jnp.int32), n_used - 1) * tm
    tile_e = jnp.minimum(jnp.searchsorted(pend, tile_start, side='right'), N_EXPERTS - 1).astype(jnp.int32)
    return tile_e, n_used.reshape(1), slot_tok, dest.reshape(T, TOP_K)


def _trunk(x3, p):
    batch, seq, _ = x3.shape
    x = x3.reshape(batch * seq, D_MODEL)
    cos, sa, sb = _rope_tables(seq)
    q, k, v, gates = _qkvg_call(x, p["ln_in_g"], p["ln_in_b"], p["w_in"], cos, sa, sb,
                                p["q_norm"], p["k_norm"], seq)
    conv_in = _conv_call(x, p["ln_in_g"], p["ln_in_b"], p["w_in"], p["conv_w"], seq)
    attn = _attention(q, k, v, p["q_norm"], p["k_norm"], batch, seq)
    merged = _merge_call(attn, conv_in, p["w_attn_out"], p["w_conv_out"], gates)
    x1, idx_pad, gate_pad = _out_call(x, p["ln_in_g"], p["ln_in_b"], merged, p["w_o"],
                                      p["ln1_g"], p["ln1_b"], p["w_router"], p["b_router"])
    tile_e, n_used, slot_tok, dest = _dispatch_plan(idx_pad[:, :TOP_K], TM_MOE)
    eout = _expert_call(tile_e, n_used, slot_tok, x1,
                        p["w_gate_up"], p["b_gate_up"], p["w_down"], p["b_down"])
    y = _combine_call(dest, eout, x1, gate_pad, p["ln2_g"], p["ln2_b"])
    return y.reshape(batch, seq, D_MODEL)


def _prepare(ln_in_g, ln_in_b, w_in, q_norm, k_norm, conv_w, w_attn_out, w_conv_out, w_o,
             ln1_g, ln1_b, w_router, b_router, w_gate_up, b_gate_up, w_down, b_down, ln2_g, ln2_b):
    assert w_in.shape[0] == DEPTH
    vec = lambda a: a.reshape(1, -1)
    return {
        "ln_in_g": vec(ln_in_g), "ln_in_b": vec(ln_in_b),
        "w_in": w_in[0].astype(BF16),
        "q_norm": vec(q_norm[0]), "k_norm": vec(k_norm[0]),
        "conv_w": conv_w[0],
        "w_attn_out": w_attn_out[0].astype(BF16),
        "w_conv_out": w_conv_out[0].astype(BF16),
        "w_o": w_o[0].astype(BF16),
        "ln1_g": vec(ln1_g[0]), "ln1_b": vec(ln1_b[0]),
        "w_router": w_router[0], "b_router": vec(b_router[0]),
        "w_gate_up": w_gate_up[0].astype(BF16),
        "b_gate_up": b_gate_up[0].reshape(N_EXPERTS, 1, 2 * D_EXPERT),
        "w_down": w_down[0].astype(BF16),
        "b_down": b_down[0].reshape(N_EXPERTS, 1, D_MODEL),
        "ln2_g": vec(ln2_g[0]), "ln2_b": vec(ln2_b[0]),
    }


def kernel(x_prompt, x_sample, ln_in_g, ln_in_b, w_in, q_norm, k_norm, conv_w, w_attn_out, w_conv_out, w_o,
           ln1_g, ln1_b, w_router, b_router, w_gate_up, b_gate_up, w_down, b_down, ln2_g, ln2_b):
    p = _prepare(ln_in_g, ln_in_b, w_in, q_norm, k_norm, conv_w, w_attn_out, w_conv_out, w_o,
                 ln1_g, ln1_b, w_router, b_router, w_gate_up, b_gate_up, w_down, b_down, ln2_g, ln2_b)
    return (_trunk(x_prompt, p), _trunk(x_sample, p))
```

```python
import functools

import jax
import jax.numpy as jnp
from jax import lax
from jax.experimental import pallas as pl
from jax.experimental.pallas import tpu as pltpu

F32 = jnp.float32
BF16 = jnp.bfloat16

D_MODEL = 2048
GRID_W = 64
N_HEADS = 16
N_KV_HEADS = 4
GROUPS = N_HEADS // N_KV_HEADS
HEAD_DIM = 128
ROT_HALF = HEAD_DIM // 2
N_FREQ = ROT_HALF // 2
ROPE_THETA = 10000.0
ATTN_WIDTH = N_HEADS * HEAD_DIM
KV_WIDTH = N_KV_HEADS * HEAD_DIM
N_EXPERTS = 32
TOP_K = 4
D_EXPERT = D_MODEL
SWIGLU_LIMIT = 7.0
SWIGLU_ALPHA = 1.702
RMS_EPS = 1e-6
LN_EPS = 1e-5
DEPTH = 1
DEEPNORM_ALPHA = (2 * DEPTH) ** 0.25
ATTN_SCALE = HEAD_DIM ** -0.5

OFF_K = ATTN_WIDTH
OFF_V = OFF_K + KV_WIDTH
OFF_CB = OFF_V + KV_WIDTH
OFF_CC = OFF_CB + D_MODEL
OFF_CX = OFF_CC + D_MODEL
OFF_GA = OFF_CX + D_MODEL
OFF_GC = OFF_GA + D_MODEL

LANES = 128
BF16_ROWS = 16
F32_ROWS = 8
VMEM_LIMIT = 56 * 1024 * 1024

TM_QKV = 1024
TM_PROJ = 512
TN_PROJ = 512
TQ_ATTN = 256
TK_ATTN = 512
TK_ATTN_BOUNDED = 1024
TM_OUT = 256
TM_MOE = 512
FC_MOE = 512
TM_COMB = 128
TM_DISPATCH = 128
HALO = BF16_ROWS


def _cparams(sem):
    return pltpu.CompilerParams(dimension_semantics=sem, vmem_limit_bytes=VMEM_LIMIT)


def _pack_bf16_pairs(x):
    half = x.shape[1] // 2
    lo = lax.bitcast_convert_type(x[:, :half].astype(BF16).astype(F32), jnp.uint32)
    hi = lax.bitcast_convert_type(x[:, half:].astype(BF16).astype(F32), jnp.uint32)
    return (lo >> 16) | (hi & jnp.uint32(0xFFFF0000))


def _unpack_bf16_pairs(w):
    lo = lax.bitcast_convert_type(w << 16, F32)
    hi = lax.bitcast_convert_type(w & jnp.uint32(0xFFFF0000), F32)
    return lo.astype(BF16), hi.astype(BF16)


def _layer_norm(x, g, b):
    mu = jnp.mean(x, axis=-1, keepdims=True)
    xc = x - mu
    var = jnp.mean(xc * xc, axis=-1, keepdims=True)
    return xc * lax.rsqrt(var + LN_EPS) * g + b


N_QT = ATTN_WIDTH // TN_PROJ
J_K = N_QT
J_V = N_QT + 1
J_G = N_QT + 2
N_GT = 2 * D_MODEL // TN_PROJ


def _qkvg_kernel(x_ref, lng_ref, lnb_ref, w_ref, cos_ref, sa_ref, sb_ref, qn_ref, kn_ref,
                 q_ref, k_ref, v_ref, g_ref, xn_ref):
    j = pl.program_id(1)

    @pl.when(j == 0)
    def _():
        xn_ref[...] = _layer_norm(x_ref[...], lng_ref[...], lnb_ref[...]).astype(BF16)

    acc = jnp.dot(xn_ref[...], w_ref[...], preferred_element_type=F32)

    def norm_rope(h, gain, scale):
        ms = jnp.mean(h * h, axis=-1, keepdims=True)
        hn = h * lax.rsqrt(ms + RMS_EPS) * gain
        r = (hn * cos_ref[...] + pltpu.roll(hn, HEAD_DIM - N_FREQ, 1) * sa_ref[...]
             + pltpu.roll(hn, N_FREQ, 1) * sb_ref[...])
        return r * scale

    @pl.when(j < J_K)
    def _():
        for a in range(TN_PROJ // HEAD_DIM):
            sl = slice(a * HEAD_DIM, (a + 1) * HEAD_DIM)
            q_ref[:, sl] = norm_rope(acc[:, sl], qn_ref[...], ATTN_SCALE).astype(BF16)

    @pl.when(j == J_K)
    def _():
        for a in range(N_KV_HEADS):
            sl = slice(a * HEAD_DIM, (a + 1) * HEAD_DIM)
            k_ref[:, sl] = norm_rope(acc[:, sl], kn_ref[...], 1.0).astype(BF16)

    @pl.when(j == J_V)
    def _():
        v_ref[...] = acc.astype(BF16)

    @pl.when(j >= J_G)
    def _():
        g_ref[...] = jax.nn.sigmoid(acc).astype(BF16)


def _qkvg_call(x, lng, lnb, w_in, cos, sa, sb, qn, kn, seq):
    T = x.shape[0]
    tm, tn = TM_QKV, TN_PROJ
    n_seq = seq // tm
    gate_blk0 = OFF_GA // tn

    def w_map(i, j):
        return (0, jnp.where(j < J_G, j, j - J_G + gate_blk0))

    rope_spec = pl.BlockSpec((tm, HEAD_DIM), lambda i, j: (i % n_seq, 0))
    vec_spec = lambda n: pl.BlockSpec((1, n), lambda i, j: (0, 0))
    return pl.pallas_call(
        _qkvg_kernel,
        grid=(T // tm, J_G + N_GT),
        in_specs=[
            pl.BlockSpec((tm, D_MODEL), lambda i, j: (i, 0)),
            vec_spec(D_MODEL), vec_spec(D_MODEL),
            pl.BlockSpec((D_MODEL, tn), w_map),
            rope_spec, rope_spec, rope_spec,
            vec_spec(HEAD_DIM), vec_spec(HEAD_DIM),
        ],
        out_specs=[
            pl.BlockSpec((tm, tn), lambda i, j: (i, jnp.minimum(j, N_QT - 1))),
            pl.BlockSpec((tm, KV_WIDTH), lambda i, j: (i, 0)),
            pl.BlockSpec((tm, KV_WIDTH), lambda i, j: (i, 0)),
            pl.BlockSpec((tm, tn), lambda i, j: (i, jnp.clip(j - J_G, 0, N_GT - 1))),
        ],
        out_shape=[
            jax.ShapeDtypeStruct((T, ATTN_WIDTH), BF16),
            jax.ShapeDtypeStruct((T, KV_WIDTH), BF16),
            jax.ShapeDtypeStruct((T, KV_WIDTH), BF16),
            jax.ShapeDtypeStruct((T, 2 * D_MODEL), BF16),
        ],
        scratch_shapes=[pltpu.VMEM((tm, D_MODEL), BF16)],
        name="qkv_gate_proj",
        compiler_params=_cparams(("arbitrary", "arbitrary")),
    )(x, lng, lnb, w_in, cos, sa, sb, qn, kn)


def _conv_kernel(x_ref, xp_ref, xq_ref, lng_ref, lnb_ref, wcb_ref, wcc_ref, wcx_ref, cw_ref,
                 o_ref, xe_ref, u_ref, *, n_seq, tm):
    i = pl.program_id(0)
    j = pl.program_id(1)

    @pl.when(j == 0)
    def _():
        g = lng_ref[...]
        b = lnb_ref[...]
        first = (i % n_seq) == 0
        last = (i % n_seq) == n_seq - 1
        prev = jnp.where(first, 0.0, _layer_norm(xp_ref[...], g, b))
        nxt = jnp.where(last, 0.0, _layer_norm(xq_ref[...], g, b))
        z = jnp.zeros((HALO - F32_ROWS, D_MODEL), F32)
        xe_ref[0:HALO, :] = jnp.concatenate([z, prev], axis=0).astype(BF16)
        xe_ref[HALO:HALO + tm, :] = _layer_norm(x_ref[...], g, b).astype(BF16)
        xe_ref[HALO + tm:2 * HALO + tm, :] = jnp.concatenate([nxt, z], axis=0).astype(BF16)

    xe = xe_ref[...]
    u_ref[...] = (jnp.dot(xe, wcc_ref[...], preferred_element_type=F32)
                  * jnp.dot(xe, wcx_ref[...], preferred_element_type=F32))
    cb = jnp.dot(xe_ref[HALO:HALO + tm, :], wcb_ref[...], preferred_element_type=F32)
    cw = cw_ref[...]
    conv = (cw[0:1, :] * u_ref[HALO - 1:HALO - 1 + tm, :]
            + cw[1:2, :] * u_ref[HALO:HALO + tm, :]
            + cw[2:3, :] * u_ref[HALO + 1:HALO + 1 + tm, :])
    o_ref[...] = (cb * conv).astype(BF16)


def _conv_call(x, lng, lnb, w_in, conv_w, seq):
    T = x.shape[0]
    tm, tn = TM_PROJ, TN_PROJ
    n_seq = seq // tm
    halo_per_tile = tm // F32_ROWS
    n_halo_blocks = T // F32_ROWS

    def w_spec(off):
        return pl.BlockSpec((D_MODEL, tn), lambda i, j: (0, off // tn + j))

    vec_spec = pl.BlockSpec((1, D_MODEL), lambda i, j: (0, 0))
    return pl.pallas_call(
        functools.partial(_conv_kernel, n_seq=n_seq, tm=tm),
        grid=(T // tm, D_MODEL // tn),
        in_specs=[
            pl.BlockSpec((tm, D_MODEL), lambda i, j: (i, 0)),
            pl.BlockSpec((F32_ROWS, D_MODEL), lambda i, j: (jnp.maximum(i * halo_per_tile - 1, 0), 0)),
            pl.BlockSpec((F32_ROWS, D_MODEL),
                         lambda i, j: (jnp.minimum((i + 1) * halo_per_tile, n_halo_blocks - 1), 0)),
            vec_spec, vec_spec,
            w_spec(OFF_CB), w_spec(OFF_CC), w_spec(OFF_CX),
            pl.BlockSpec((3, tn), lambda i, j: (0, j)),
        ],
        out_specs=pl.BlockSpec((tm, tn), lambda i, j: (i, j)),
        out_shape=jax.ShapeDtypeStruct((T, D_MODEL), BF16),
        scratch_shapes=[pltpu.VMEM((tm + 2 * HALO, D_MODEL), BF16),
                        pltpu.VMEM((tm + 2 * HALO, tn), F32)],
        name="conv_proj",
        compiler_params=_cparams(("arbitrary", "arbitrary")),
    )(x, x, x, lng, lnb, w_in, w_in, w_in, conv_w)


def _attn_kernel(q_ref, k_ref, v_ref, o_ref, m_ref, l_ref, acc_ref, *, tq, tk, n_kv):
    q4 = jnp.concatenate([q_ref[:, g * HEAD_DIM:(g + 1) * HEAD_DIM] for g in range(GROUPS)], axis=0)
    m_ref[...] = jnp.full(m_ref.shape, -jnp.inf, F32)
    l_ref[...] = jnp.zeros(l_ref.shape, F32)
    acc_ref[...] = jnp.zeros(acc_ref.shape, F32)

    def body(c, carry):
        start = pl.multiple_of(c * tk, tk)
        kc = k_ref[pl.ds(start, tk), :]
        vc = v_ref[pl.ds(start, tk), :]
        s = lax.dot_general(q4, kc, (((1,), (1,)), ((), ())), preferred_element_type=F32)
        m_prev = m_ref[...]
        m_new = jnp.maximum(m_prev, jnp.max(s, axis=-1, keepdims=True))
        p = jnp.exp(s - m_new)
        alpha = jnp.exp(m_prev - m_new)
        l_ref[...] = alpha * l_ref[...] + jnp.sum(p, axis=-1, keepdims=True)
        acc_ref[...] = alpha * acc_ref[...] + jnp.dot(p.astype(BF16), vc, preferred_element_type=F32)
        m_ref[...] = m_new
        return carry

    lax.fori_loop(0, n_kv, body, 0)
    out = acc_ref[...] / l_ref[...]
    for g in range(GROUPS):
        o_ref[:, g * HEAD_DIM:(g + 1) * HEAD_DIM] = out[g * tq:(g + 1) * tq, :].astype(BF16)


def _attn_call(q, k, v, batch, seq):
    T = q.shape[0]
    tq = min(TQ_ATTN, seq)
    tk = min(TK_ATTN, seq)
    n_q = seq // tq
    gw = GROUPS * HEAD_DIM
    return pl.pallas_call(
        functools.partial(_attn_kernel, tq=tq, tk=tk, n_kv=seq // tk),
        grid=(batch, N_KV_HEADS, n_q),
        in_specs=[
            pl.BlockSpec((tq, gw), lambda b, h, i: (b * n_q + i, h)),
            pl.BlockSpec((seq, HEAD_DIM), lambda b, h, i: (b, h)),
            pl.BlockSpec((seq, HEAD_DIM), lambda b, h, i: (b, h)),
        ],
        out_specs=pl.BlockSpec((tq, gw), lambda b, h, i: (b * n_q + i, h)),
        out_shape=jax.ShapeDtypeStruct((T, ATTN_WIDTH), BF16),
        scratch_shapes=[pltpu.VMEM((GROUPS * tq, 1), F32),
                        pltpu.VMEM((GROUPS * tq, 1), F32),
                        pltpu.VMEM((GROUPS * tq, HEAD_DIM), F32)],
        name="gqa_attention",
        compiler_params=_cparams(("arbitrary", "arbitrary", "arbitrary")),
    )(q, k, v)


def _attn_bounded_kernel(nm_ref, q_ref, k_ref, v_ref, o_ref, ka_ref, va_ref, acc_ref, *, tq, tk, n_kv):
    @pl.when(pl.program_id(2) == 0)
    def _():
        lane = lax.broadcasted_iota(jnp.int32, k_ref.shape, 1)
        ka_ref[:, :HEAD_DIM] = k_ref[...]
        ka_ref[:, HEAD_DIM:] = jnp.where(lane == 0, 1.0, 0.0).astype(BF16)
        va_ref[:, :HEAD_DIM] = v_ref[...]
        va_ref[:, HEAD_DIM:] = jnp.ones(v_ref.shape, BF16)

    lane = lax.broadcasted_iota(jnp.int32, (tq, HEAD_DIM), 1)
    shift = jnp.where(lane == 0, nm_ref[0], 0.0).astype(BF16)
    q4 = jnp.concatenate(
        [jnp.concatenate([q_ref[:, g * HEAD_DIM:(g + 1) * HEAD_DIM], shift], axis=1) for g in range(GROUPS)],
        axis=0)
    acc_ref[...] = jnp.zeros(acc_ref.shape, F32)

    def body(c, carry):
        start = pl.multiple_of(c * tk, tk)
        s = lax.dot_general(q4, ka_ref[pl.ds(start, tk), :], (((1,), (1,)), ((), ())),
                            preferred_element_type=F32)
        p = jnp.exp(s).astype(BF16)
        acc_ref[...] += jnp.dot(p, va_ref[pl.ds(start, tk), :], preferred_element_type=F32)
        return carry

    lax.fori_loop(0, n_kv, body, 0, unroll=True)
    out = acc_ref[:, :HEAD_DIM] / acc_ref[:, HEAD_DIM:]
    for g in range(GROUPS):
        o_ref[:, g * HEAD_DIM:(g + 1) * HEAD_DIM] = out[g * tq:(g + 1) * tq, :].astype(BF16)


def _attn_bounded_call(neg_shift, q, k, v, batch, seq):
    T = q.shape[0]
    tq = min(TQ_ATTN, seq)
    tk = min(TK_ATTN_BOUNDED, seq)
    n_q = seq // tq
    gw = GROUPS * HEAD_DIM
    return pl.pallas_call(
        functools.partial(_attn_bounded_kernel, tq=tq, tk=tk, n_kv=seq // tk),
        grid=(batch, N_KV_HEADS, n_q),
        in_specs=[
            pl.BlockSpec(memory_space=pltpu.SMEM),
            pl.BlockSpec((tq, gw), lambda b, h, i: (b * n_q + i, h)),
            pl.BlockSpec((seq, HEAD_DIM), lambda b, h, i: (b, h)),
            pl.BlockSpec((seq, HEAD_DIM), lambda b, h, i: (b, h)),
        ],
        out_specs=pl.BlockSpec((tq, gw), lambda b, h, i: (b * n_q + i, h)),
        out_shape=jax.ShapeDtypeStruct((T, ATTN_WIDTH), BF16),
        scratch_shapes=[pltpu.VMEM((seq, 2 * HEAD_DIM), BF16),
                        pltpu.VMEM((seq, 2 * HEAD_DIM), BF16),
                        pltpu.VMEM((GROUPS * tq, 2 * HEAD_DIM), F32)],
        name="gqa_attention_bounded",
        compiler_params=_cparams(("arbitrary", "arbitrary", "arbitrary")),
    )(neg_shift, q, k, v)


MAX_SCORE_SHIFT = 40.0


def _score_bound(q_norm, k_norm):
    slack = 1.02
    return slack * HEAD_DIM * ATTN_SCALE * jnp.max(jnp.abs(q_norm)) * jnp.max(jnp.abs(k_norm))


def _attention(q, k, v, q_norm, k_norm, batch, seq):
    m = _score_bound(q_norm, k_norm)
    return lax.cond(
        m <= MAX_SCORE_SHIFT,
        lambda: _attn_bounded_call((-m).reshape(1).astype(F32), q, k, v, batch, seq),
        lambda: _attn_call(q, k, v, batch, seq))


def _mix_out_kernel(x_ref, lng_ref, lnb_ref, a_ref, c_ref, g_ref, wa_ref, wc_ref, wo_ref, g1_ref, b1_ref,
                    wrh_ref, wrl_ref, br_ref, x1_ref, x1p_ref, idx_ref, gate_ref):
    att = jnp.dot(a_ref[...], wa_ref[...], preferred_element_type=F32)
    cnv = jnp.dot(c_ref[...], wc_ref[...], preferred_element_type=F32)
    merged = (g_ref[:, :D_MODEL].astype(F32) * att + g_ref[:, D_MODEL:].astype(F32) * cnv).astype(BF16)
    out = jnp.dot(merged, wo_ref[...], preferred_element_type=F32)
    xn = _layer_norm(x_ref[...], lng_ref[...], lnb_ref[...])
    x1 = _layer_norm(DEEPNORM_ALPHA * xn + out, g1_ref[...], b1_ref[...])
    x1_ref[...] = x1
    x1p_ref[...] = _pack_bf16_pairs(x1)
    x_hi = x1.astype(BF16)
    x_lo = (x1 - x_hi.astype(F32)).astype(BF16)
    logits = (jnp.dot(x_hi, wrh_ref[...], preferred_element_type=F32)
              + jnp.dot(x_hi, wrl_ref[...], preferred_element_type=F32)
              + jnp.dot(x_lo, wrh_ref[...], preferred_element_type=F32)) + br_ref[...]
    lane = lax.broadcasted_iota(jnp.int32, logits.shape, 1).astype(F32)
    vals = logits
    tops, ids = [], []
    for _ in range(TOP_K):
        mx = jnp.max(vals, axis=-1, keepdims=True)
        ix = jnp.min(jnp.where(vals == mx, lane, float(N_EXPERTS)), axis=-1, keepdims=True)
        tops.append(mx)
        ids.append(ix)
        vals = jnp.where(lane == ix, -jnp.inf, vals)
    exps = [jnp.exp(t - tops[0]) for t in tops]
    den = exps[0] + exps[1] + exps[2] + exps[3]
    out_lane = lax.broadcasted_iota(jnp.int32, idx_ref.shape, 1)
    idx_out = jnp.zeros(idx_ref.shape, F32)
    gate_out = jnp.zeros(gate_ref.shape, F32)
    for kk in range(TOP_K):
        idx_out = jnp.where(out_lane == kk, ids[kk], idx_out)
        gate_out = jnp.where(out_lane == kk, exps[kk] / den, gate_out)
    idx_ref[...] = idx_out.astype(jnp.int32)
    gate_ref[...] = gate_out


def _mix_out_call(x, lng, lnb, attn, conv_in, gates, wa, wc, wo, g1, b1, wr_hi, wr_lo, br):
    T = x.shape[0]
    tm = TM_OUT
    row = lambda n: pl.BlockSpec((tm, n), lambda i: (i, 0))
    vec = lambda n: pl.BlockSpec((1, n), lambda i: (0, 0))
    resident = lambda r, c: pl.BlockSpec((r, c), lambda i: (0, 0), pipeline_mode=pl.Buffered(1))
    return pl.pallas_call(
        _mix_out_kernel,
        grid=(T // tm,),
        in_specs=[row(D_MODEL), vec(D_MODEL), vec(D_MODEL),
                  row(ATTN_WIDTH), row(D_MODEL), row(2 * D_MODEL),
                  resident(ATTN_WIDTH, D_MODEL), resident(D_MODEL, D_MODEL), resident(D_MODEL, D_MODEL),
                  vec(D_MODEL), vec(D_MODEL),
                  pl.BlockSpec((D_MODEL, N_EXPERTS), lambda i: (0, 0)),
                  pl.BlockSpec((D_MODEL, N_EXPERTS), lambda i: (0, 0)),
                  vec(N_EXPERTS)],
        out_specs=[row(D_MODEL), row(D_MODEL // 2), row(LANES), row(LANES)],
        out_shape=[jax.ShapeDtypeStruct((T, D_MODEL), F32),
                   jax.ShapeDtypeStruct((T, D_MODEL // 2), jnp.uint32),
                   jax.ShapeDtypeStruct((T, LANES), jnp.int32),
                   jax.ShapeDtypeStruct((T, LANES), F32)],
        name="mix_out_router",
        compiler_params=_cparams(("arbitrary",)),
    )(x, lng, lnb, attn, conv_in, gates, wa, wc, wo, g1, b1, wr_hi, wr_lo, br)


def _dispatch_copy(stage, par, r, dst_hbm, slot_row, sem):
    return pltpu.make_async_copy(stage.at[par, pl.ds(r, 1), :], dst_hbm.at[pl.ds(slot_row, 1), :], sem.at[par])


def _dispatch_wait(stage, dst_hbm, sem, par, n):
    def body(r, c):
        _dispatch_copy(stage, par, 0, dst_hbm, 0, sem).wait()
        return c
    lax.fori_loop(0, n, body, 0, unroll=8)


def _dispatch_kernel(dest_ref, x_ref, xs_in_hbm, xs_hbm, stage, sem, *, tm, n_tiles):
    del xs_in_hbm
    i = pl.program_id(0)
    par = i % 2
    n = TOP_K * tm

    @pl.when(i >= 2)
    def _():
        _dispatch_wait(stage, xs_hbm, sem, par, n)

    stage[par] = x_ref[...]
    for r in range(tm):
        for kk in range(TOP_K):
            _dispatch_copy(stage, par, r, xs_hbm, dest_ref[0, 0, r * TOP_K + kk], sem).start()

    @pl.when(i == n_tiles - 1)
    def _():
        _dispatch_wait(stage, xs_hbm, sem, par, n)

        @pl.when(i >= 1)
        def _():
            _dispatch_wait(stage, xs_hbm, sem, 1 - par, n)


def _dispatch_call(dest, x1p, xs):
    T, width = x1p.shape
    tm = TM_DISPATCH
    n_tiles = T // tm
    d3 = dest.reshape(n_tiles, 1, TOP_K * tm)
    return pl.pallas_call(
        functools.partial(_dispatch_kernel, tm=tm, n_tiles=n_tiles),
        grid=(n_tiles,),
        in_specs=[
            pl.BlockSpec((1, 1, TOP_K * tm), lambda i: (i, 0, 0), memory_space=pltpu.SMEM),
            pl.BlockSpec((tm, width), lambda i: (i, 0)),
            pl.BlockSpec(memory_space=pl.ANY),
        ],
        out_specs=pl.BlockSpec(memory_space=pl.ANY),
        out_shape=jax.ShapeDtypeStruct(xs.shape, xs.dtype),
        scratch_shapes=[pltpu.VMEM((2, tm, width), x1p.dtype), pltpu.SemaphoreType.DMA((2,))],
        input_output_aliases={2: 0},
        name="moe_dispatch",
        compiler_params=_cparams(("arbitrary",)),
    )(d3, x1p, xs)


def _expert_kernel(te_ref, nu_ref, xs_ref, wg_ref, wu_ref, bg_ref, bu_ref, wd_ref, bd_ref, o_ref, xb_ref,
                   *, n_fc):
    i = pl.program_id(0)
    j = pl.program_id(1)
    n_used = nu_ref[0]
    half = D_MODEL // 2

    @pl.when(i < n_used)
    def _():
        @pl.when(j == 0)
        def _():
            lo, hi = _unpack_bf16_pairs(xs_ref[...])
            xb_ref[:, :half] = lo
            xb_ref[:, half:] = hi
            o_ref[...] = jnp.broadcast_to(bd_ref[...], o_ref.shape)

        xb = xb_ref[...]
        hg = jnp.dot(xb, wg_ref[...], preferred_element_type=F32) + bg_ref[...]
        hu = jnp.dot(xb, wu_ref[...], preferred_element_type=F32) + bu_ref[...]
        g = jnp.minimum(hg, SWIGLU_LIMIT)
        u = jnp.clip(hu, -SWIGLU_LIMIT, SWIGLU_LIMIT)
        act = (u + 1.0) * (g * jax.nn.sigmoid(SWIGLU_ALPHA * g))
        o_ref[...] += jnp.dot(act.astype(BF16), wd_ref[...], preferred_element_type=F32)

    @pl.when(jnp.logical_and(i >= n_used, j == 0))
    def _():
        o_ref[...] = jnp.zeros(o_ref.shape, F32)


def _expert_call(tile_e, n_used, xs, wgu, bgu, wd, bd):
    n_slots = xs.shape[0]
    tm, fc = TM_MOE, FC_MOE
    n_tiles = n_slots // tm
    n_fc = D_EXPERT // fc

    def live(i, nu):
        return jnp.minimum(i, nu[0] - 1)

    def jj(i, j, nu):
        return jnp.where(i < nu[0], j, n_fc - 1)

    grid_spec = pltpu.PrefetchScalarGridSpec(
        num_scalar_prefetch=2,
        grid=(n_tiles, n_fc),
        in_specs=[
            pl.BlockSpec((tm, D_MODEL // 2), lambda i, j, te, nu: (live(i, nu), 0)),
            pl.BlockSpec((None, D_MODEL, fc), lambda i, j, te, nu: (te[i], 0, jj(i, j, nu))),
            pl.BlockSpec((None, D_MODEL, fc), lambda i, j, te, nu: (te[i], 0, n_fc + jj(i, j, nu))),
            pl.BlockSpec((None, 1, fc), lambda i, j, te, nu: (te[i], 0, jj(i, j, nu))),
            pl.BlockSpec((None, 1, fc), lambda i, j, te, nu: (te[i], 0, n_fc + jj(i, j, nu))),
            pl.BlockSpec((None, fc, D_MODEL), lambda i, j, te, nu: (te[i], jj(i, j, nu), 0)),
            pl.BlockSpec((None, 1, D_MODEL), lambda i, j, te, nu: (te[i], 0, 0)),
        ],
        out_specs=pl.BlockSpec((tm, D_MODEL), lambda i, j, te, nu: (i, 0)),
        scratch_shapes=[pltpu.VMEM((tm, D_MODEL), BF16)],
    )
    return pl.pallas_call(
        functools.partial(_expert_kernel, n_fc=n_fc),
        grid_spec=grid_spec,
        out_shape=jax.ShapeDtypeStruct((n_slots, D_MODEL), F32),
        name="expert_ffn",
        compiler_params=_cparams(("arbitrary", "arbitrary")),
    )(tile_e, n_used, xs, wgu, wgu, bgu, bgu, wd, bd)


def _row_copy(src_hbm, row, buf, slot, r, sem):
    return pltpu.make_async_copy(src_hbm.at[pl.ds(row, 1), :], buf.at[slot, pl.ds(r, 1), :], sem.at[slot])


def _start_rows(idx_ref, src_hbm, buf, slot, sem, n_rows):
    for r in range(n_rows):
        _row_copy(src_hbm, idx_ref[0, 0, r], buf, slot, r, sem).start()


def _wait_rows(src_hbm, buf, slot, sem, n_rows):
    def body(r, c):
        _row_copy(src_hbm, 0, buf, slot, r, sem).wait()
        return c
    lax.fori_loop(0, n_rows, body, 0, unroll=8)


def _combine_kernel(d0_ref, dn_ref, e_hbm, x1_ref, gate_ref, g2_ref, b2_ref, y_ref, buf, sem, *, tm, n_tiles):
    i = pl.program_id(0)
    slot = i % 2
    n_rows = TOP_K * tm

    @pl.when(i == 0)
    def _():
        _start_rows(d0_ref, e_hbm, buf, 0, sem, n_rows)

    @pl.when(i + 1 < n_tiles)
    def _():
        _start_rows(dn_ref, e_hbm, buf, 1 - slot, sem, n_rows)

    _wait_rows(e_hbm, buf, slot, sem, n_rows)
    gate = gate_ref[...]
    ffn = gate[:, 0:1] * buf[slot, 0:tm, :]
    for kk in range(1, TOP_K):
        ffn = ffn + gate[:, kk:kk + 1] * buf[slot, kk * tm:(kk + 1) * tm, :]
    y_ref[...] = _layer_norm(DEEPNORM_ALPHA * x1_ref[...] + ffn, g2_ref[...], b2_ref[...])


def _combine_call(dest, eout, x1, gate_pad, g2, b2):
    T = x1.shape[0]
    tm = TM_COMB
    n_tiles = T // tm
    d3 = dest.reshape(n_tiles, tm, TOP_K).transpose(0, 2, 1).reshape(n_tiles, 1, TOP_K * tm)
    smem_blk = lambda imap: pl.BlockSpec((1, 1, TOP_K * tm), imap, memory_space=pltpu.SMEM)
    vec = pl.BlockSpec((1, D_MODEL), lambda i: (0, 0))
    return pl.pallas_call(
        functools.partial(_combine_kernel, tm=tm, n_tiles=n_tiles),
        grid=(n_tiles,),
        in_specs=[
            smem_blk(lambda i: (0, 0, 0)),
            smem_blk(lambda i: (jnp.minimum(i + 1, n_tiles - 1), 0, 0)),
            pl.BlockSpec(memory_space=pl.ANY),
            pl.BlockSpec((tm, D_MODEL), lambda i: (i, 0)),
            pl.BlockSpec((tm, LANES), lambda i: (i, 0)),
            vec, vec,
        ],
        out_specs=pl.BlockSpec((tm, D_MODEL), lambda i: (i, 0)),
        out_shape=jax.ShapeDtypeStruct((T, D_MODEL), F32),
        scratch_shapes=[pltpu.VMEM((2, TOP_K * tm, D_MODEL), F32),
                        pltpu.SemaphoreType.DMA((2,))],
        name="combine_ln",
        compiler_params=_cparams(("arbitrary",)),
    )(d3, d3, eout, x1, gate_pad, g2, b2)


def _rope_tables(seq):
    pos = jnp.arange(seq, dtype=jnp.int32)
    row = (pos // GRID_W).astype(F32)
    col = (pos % GRID_W).astype(F32)
    inv_freq = ROPE_THETA ** (-jnp.arange(N_FREQ, dtype=F32) / N_FREQ)
    ang_r = row[:, None] * inv_freq[None, :]
    ang_c = col[:, None] * inv_freq[None, :]
    zero = jnp.zeros_like(ang_r)
    cos = jnp.concatenate([jnp.cos(ang_r)] * 2 + [jnp.cos(ang_c)] * 2, axis=-1)
    sa = jnp.concatenate([-jnp.sin(ang_r), zero, -jnp.sin(ang_c), zero], axis=-1)
    sb = jnp.concatenate([zero, jnp.sin(ang_r), zero, jnp.sin(ang_c)], axis=-1)
    return cos, sa, sb


def _dispatch_plan(idx, tm):
    T = idx.shape[0]
    A = T * TOP_K
    n_tiles = A // tm + N_EXPERTS
    flat_e = idx.reshape(A)
    onehot = (flat_e[:, None] == jnp.arange(N_EXPERTS, dtype=jnp.int32)[None, :]).astype(jnp.int32)
    csum = jnp.cumsum(onehot, axis=0)
    counts = csum[-1]
    rank = jnp.sum(onehot * csum, axis=1) - 1
    padded = (counts + tm - 1) // tm * tm
    pend = jnp.cumsum(padded)
    pstart = pend - padded
    dest = jnp.sum(onehot * pstart[None, :], axis=1) + rank
    n_used = (pend[-1] // tm).astype(jnp.int32)
    tile_start = jnp.minimum(jnp.arange(n_tiles, dtype=jnp.int32), n_used - 1) * tm
    tile_e = jnp.minimum(jnp.searchsorted(pend, tile_start, side='right'), N_EXPERTS - 1).astype(jnp.int32)
    return tile_e, n_used.reshape(1), dest.reshape(T, TOP_K), n_tiles * tm


def _mixers(x3, p):
    batch, seq, _ = x3.shape
    x = x3.reshape(batch * seq, D_MODEL)
    cos, sa, sb = _rope_tables(seq)
    q, k, v, gates = _qkvg_call(x, p["ln_in_g"], p["ln_in_b"], p["w_in"], cos, sa, sb,
                                p["q_norm"], p["k_norm"], seq)
    conv_in = _conv_call(x, p["ln_in_g"], p["ln_in_b"], p["w_in"], p["conv_w"], seq)
    attn = _attention(q, k, v, p["q_norm"], p["k_norm"], batch, seq)
    return _mix_out_call(x, p["ln_in_g"], p["ln_in_b"], attn, conv_in, gates,
                         p["w_attn_out"], p["w_conv_out"], p["w_o"],
                         p["ln1_g"], p["ln1_b"], p["w_router_hi"], p["w_router_lo"], p["b_router"])


def _trunks(xs3, p):
    mixed = [_mixers(x3, p) for x3 in xs3]
    idx_all = jnp.concatenate([m[2][:, :TOP_K] for m in mixed], axis=0)
    tile_e, n_used, dest_all, n_slots = _dispatch_plan(idx_all, TM_MOE)
    slots = jnp.zeros((n_slots, D_MODEL // 2), jnp.uint32)
    row0 = 0
    dests = []
    for x1, x1p, _, _ in mixed:
        dests.append(dest_all[row0:row0 + x1.shape[0]])
        slots = _dispatch_call(dests[-1], x1p, slots)
        row0 += x1.shape[0]
    eout = _expert_call(tile_e, n_used, slots, p["w_gate_up"], p["b_gate_up"], p["w_down"], p["b_down"])
    outs = []
    for x3, (x1, _, _, gate_pad), dest in zip(xs3, mixed, dests):
        y = _combine_call(dest, eout, x1, gate_pad, p["ln2_g"], p["ln2_b"])
        outs.append(y.reshape(x3.shape))
    return tuple(outs)


def _prepare(ln_in_g, ln_in_b, w_in, q_norm, k_norm, conv_w, w_attn_out, w_conv_out, w_o,
             ln1_g, ln1_b, w_router, b_router, w_gate_up, b_gate_up, w_down, b_down, ln2_g, ln2_b):
    assert w_in.shape[0] == DEPTH
    vec = lambda a: a.reshape(1, -1)
    return {
        "ln_in_g": vec(ln_in_g), "ln_in_b": vec(ln_in_b),
        "w_in": w_in[0].astype(BF16),
        "q_norm": vec(q_norm[0]), "k_norm": vec(k_norm[0]),
        "conv_w": conv_w[0],
        "w_attn_out": w_attn_out[0].astype(BF16),
        "w_conv_out": w_conv_out[0].astype(BF16),
        "w_o": w_o[0].astype(BF16),
        "ln1_g": vec(ln1_g[0]), "ln1_b": vec(ln1_b[0]),
        "w_router_hi": w_router[0].astype(BF16),
        "w_router_lo": (w_router[0] - w_router[0].astype(BF16).astype(F32)).astype(BF16),
        "b_router": vec(b_router[0]),
        "w_gate_up": w_gate_up[0].astype(BF16),
        "b_gate_up": b_gate_up[0].reshape(N_EXPERTS, 1, 2 * D_EXPERT),
        "w_down": w_down[0].astype(BF16),
        "b_down": b_down[0].reshape(N_EXPERTS, 1, D_MODEL),
        "ln2_g": vec(ln2_g[0]), "ln2_b": vec(ln2_b[0]),
    }


def kernel(x_prompt, x_sample, ln_in_g, ln_in_b, w_in, q_norm, k_norm, conv_w, w_attn_out, w_conv_out, w_o,
           ln1_g, ln1_b, w_router, b_router, w_gate_up, b_gate_up, w_down, b_down, ln2_g, ln2_b):
    p = _prepare(ln_in_g, ln_in_b, w_in, q_norm, k_norm, conv_w, w_attn_out, w_conv_out, w_o,
                 ln1_g, ln1_b, w_router, b_router, w_gate_up, b_gate_up, w_down, b_down, ln2_g, ln2_b)
    return _trunks((x_prompt, x_sample), p)
```

```python
import functools

import jax
import jax.numpy as jnp
from jax import lax
from jax.experimental import pallas as pl
from jax.experimental.pallas import tpu as pltpu

F32 = jnp.float32
BF16 = jnp.bfloat16

D_MODEL = 2048
GRID_W = 64
N_HEADS = 16
N_KV_HEADS = 4
GROUPS = N_HEADS // N_KV_HEADS
HEAD_DIM = 128
ROT_HALF = HEAD_DIM // 2
N_FREQ = ROT_HALF // 2
ROPE_THETA = 10000.0
ATTN_WIDTH = N_HEADS * HEAD_DIM
KV_WIDTH = N_KV_HEADS * HEAD_DIM
N_EXPERTS = 32
TOP_K = 4
D_EXPERT = D_MODEL
SWIGLU_LIMIT = 7.0
SWIGLU_ALPHA = 1.702
RMS_EPS = 1e-6
LN_EPS = 1e-5
DEPTH = 1
DEEPNORM_ALPHA = (2 * DEPTH) ** 0.25
ATTN_SCALE = HEAD_DIM ** -0.5

OFF_K = ATTN_WIDTH
OFF_V = OFF_K + KV_WIDTH
OFF_CB = OFF_V + KV_WIDTH
OFF_CC = OFF_CB + D_MODEL
OFF_CX = OFF_CC + D_MODEL
OFF_GA = OFF_CX + D_MODEL
OFF_GC = OFF_GA + D_MODEL

LANES = 128
BF16_ROWS = 16
F32_ROWS = 8
VMEM_LIMIT = 56 * 1024 * 1024

TM_QKV = 1024
TM_PROJ = 512
TN_PROJ = 512
TQ_ATTN = 256
TK_ATTN = 512
TK_ATTN_BOUNDED = 1024
TM_OUT = 256
TM_MOE = 512
FC_MOE = 512
TM_COMB = 128
TM_DISPATCH = 128
HALO = BF16_ROWS


def _cparams(sem):
    return pltpu.CompilerParams(dimension_semantics=sem, vmem_limit_bytes=VMEM_LIMIT)


def _pack_bf16_pairs(x):
    half = x.shape[1] // 2
    lo = lax.bitcast_convert_type(x[:, :half].astype(BF16).astype(F32), jnp.uint32)
    hi = lax.bitcast_convert_type(x[:, half:].astype(BF16).astype(F32), jnp.uint32)
    return (lo >> 16) | (hi & jnp.uint32(0xFFFF0000))


def _unpack_pairs_f32(w):
    lo = lax.bitcast_convert_type(w << 16, F32)
    hi = lax.bitcast_convert_type(w & jnp.uint32(0xFFFF0000), F32)
    return lo, hi


def _unpack_bf16_pairs(w):
    lo, hi = _unpack_pairs_f32(w)
    return lo.astype(BF16), hi.astype(BF16)


def _layer_norm(x, g, b):
    mu = jnp.mean(x, axis=-1, keepdims=True)
    xc = x - mu
    var = jnp.mean(xc * xc, axis=-1, keepdims=True)
    return xc * lax.rsqrt(var + LN_EPS) * g + b


N_QT = ATTN_WIDTH // TN_PROJ
J_K = N_QT
J_V = N_QT + 1
J_G = N_QT + 2
N_GT = 2 * D_MODEL // TN_PROJ
N_J = J_G + N_GT


def _qkvg_kernel(x_ref, lng_ref, lnb_ref, w_ref, cos_ref, sa_ref, sb_ref, qn_ref, kn_ref,
                 q_ref, k_ref, v_ref, g_ref, xn_ref, acc0_ref, acc1_ref):
    j = pl.program_id(1)
    accs = (acc0_ref, acc1_ref)

    def norm_rope(h, gain, scale):
        ms = jnp.mean(h * h, axis=-1, keepdims=True)
        hn = h * lax.rsqrt(ms + RMS_EPS) * gain
        r = (hn * cos_ref[...] + pltpu.roll(hn, HEAD_DIM - N_FREQ, 1) * sa_ref[...]
             + pltpu.roll(hn, N_FREQ, 1) * sb_ref[...])
        return r * scale

    def finish(jj, acc):
        if jj < J_K:
            for a in range(TN_PROJ // HEAD_DIM):
                sl = slice(a * HEAD_DIM, (a + 1) * HEAD_DIM)
                q_ref[:, sl] = norm_rope(acc[:, sl], qn_ref[...], ATTN_SCALE).astype(BF16)
        elif jj == J_K:
            for a in range(N_KV_HEADS):
                sl = slice(a * HEAD_DIM, (a + 1) * HEAD_DIM)
                k_ref[:, sl] = norm_rope(acc[:, sl], kn_ref[...], 1.0).astype(BF16)
        elif jj == J_V:
            for a in range(N_KV_HEADS):
                sl = slice(a * HEAD_DIM, (a + 1) * HEAD_DIM)
                v_ref[:, sl] = acc[:, sl].astype(BF16)
        else:
            for a in range(TN_PROJ // HEAD_DIM):
                sl = slice(a * HEAD_DIM, (a + 1) * HEAD_DIM)
                g_ref[:, sl] = jax.nn.sigmoid(acc[:, sl]).astype(BF16)

    for step in range(N_J + 1):
        @pl.when(j == step)
        def _(step=step):
            if step == 0:
                xn_ref[...] = _layer_norm(x_ref[...], lng_ref[...], lnb_ref[...]).astype(BF16)
            if step < N_J:
                accs[step % 2][...] = jnp.dot(xn_ref[...], w_ref[...], preferred_element_type=F32)
            if step >= 1:
                finish(step - 1, accs[(step - 1) % 2])


def _qkvg_call(x, lng, lnb, w_in, cos, sa, sb, qn, kn, seq):
    T = x.shape[0]
    tm, tn = TM_QKV, TN_PROJ
    n_seq = seq // tm
    gate_blk0 = OFF_GA // tn

    def w_map(i, j):
        jc = jnp.minimum(j, N_J - 1)
        return (0, jnp.where(jc < J_G, jc, jc - J_G + gate_blk0))

    rope_spec = pl.BlockSpec((tm, HEAD_DIM), lambda i, j: (i % n_seq, 0))
    vec_spec = lambda n: pl.BlockSpec((1, n), lambda i, j: (0, 0))
    return pl.pallas_call(
        _qkvg_kernel,
        grid=(T // tm, N_J + 1),
        in_specs=[
            pl.BlockSpec((tm, D_MODEL), lambda i, j: (i, 0)),
            vec_spec(D_MODEL), vec_spec(D_MODEL),
            pl.BlockSpec((D_MODEL, tn), w_map),
            rope_spec, rope_spec, rope_spec,
            vec_spec(HEAD_DIM), vec_spec(HEAD_DIM),
        ],
        out_specs=[
            pl.BlockSpec((tm, tn), lambda i, j: (i, jnp.clip(j - 1, 0, N_QT - 1))),
            pl.BlockSpec((tm, KV_WIDTH), lambda i, j: (i, 0)),
            pl.BlockSpec((tm, KV_WIDTH), lambda i, j: (i, 0)),
            pl.BlockSpec((tm, tn), lambda i, j: (i, jnp.clip(j - 1 - J_G, 0, N_GT - 1))),
        ],
        out_shape=[
            jax.ShapeDtypeStruct((T, ATTN_WIDTH), BF16),
            jax.ShapeDtypeStruct((T, KV_WIDTH), BF16),
            jax.ShapeDtypeStruct((T, KV_WIDTH), BF16),
            jax.ShapeDtypeStruct((T, 2 * D_MODEL), BF16),
        ],
        scratch_shapes=[pltpu.VMEM((tm, D_MODEL), BF16), pltpu.VMEM((tm, tn), F32), pltpu.VMEM((tm, tn), F32)],
        name="qkv_gate_proj",
        compiler_params=_cparams(("arbitrary", "arbitrary")),
    )(x, lng, lnb, w_in, cos, sa, sb, qn, kn)


def _conv_kernel(x_ref, xp_ref, xq_ref, lng_ref, lnb_ref, wcb_ref, wcc_ref, wcx_ref, cw_ref,
                 o_ref, xe_ref, u_ref, *, n_seq, tm):
    i = pl.program_id(0)
    j = pl.program_id(1)

    @pl.when(j == 0)
    def _():
        g = lng_ref[...]
        b = lnb_ref[...]
        first = (i % n_seq) == 0
        last = (i % n_seq) == n_seq - 1
        prev = jnp.where(first, 0.0, _layer_norm(xp_ref[...], g, b))
        nxt = jnp.where(last, 0.0, _layer_norm(xq_ref[...], g, b))
        z = jnp.zeros((HALO - F32_ROWS, D_MODEL), F32)
        xe_ref[0:HALO, :] = jnp.concatenate([z, prev], axis=0).astype(BF16)
        xe_ref[HALO:HALO + tm, :] = _layer_norm(x_ref[...], g, b).astype(BF16)
        xe_ref[HALO + tm:2 * HALO + tm, :] = jnp.concatenate([nxt, z], axis=0).astype(BF16)

    xe = xe_ref[...]
    u_ref[...] = (jnp.dot(xe, wcc_ref[...], preferred_element_type=F32)
                  * jnp.dot(xe, wcx_ref[...], preferred_element_type=F32))
    cb = jnp.dot(xe_ref[HALO:HALO + tm, :], wcb_ref[...], preferred_element_type=F32)
    cw = cw_ref[...]
    conv = (cw[0:1, :] * u_ref[HALO - 1:HALO - 1 + tm, :]
            + cw[1:2, :] * u_ref[HALO:HALO + tm, :]
            + cw[2:3, :] * u_ref[HALO + 1:HALO + 1 + tm, :])
    o_ref[...] = (cb * conv).astype(BF16)


def _conv_call(x, lng, lnb, w_in, conv_w, seq):
    T = x.shape[0]
    tm, tn = TM_PROJ, TN_PROJ
    n_seq = seq // tm
    halo_per_tile = tm // F32_ROWS
    n_halo_blocks = T // F32_ROWS

    def w_spec(off):
        return pl.BlockSpec((D_MODEL, tn), lambda i, j: (0, off // tn + j))

    vec_spec = pl.BlockSpec((1, D_MODEL), lambda i, j: (0, 0))
    return pl.pallas_call(
        functools.partial(_conv_kernel, n_seq=n_seq, tm=tm),
        grid=(T // tm, D_MODEL // tn),
        in_specs=[
            pl.BlockSpec((tm, D_MODEL), lambda i, j: (i, 0)),
            pl.BlockSpec((F32_ROWS, D_MODEL), lambda i, j: (jnp.maximum(i * halo_per_tile - 1, 0), 0)),
            pl.BlockSpec((F32_ROWS, D_MODEL),
                         lambda i, j: (jnp.minimum((i + 1) * halo_per_tile, n_halo_blocks - 1), 0)),
            vec_spec, vec_spec,
            w_spec(OFF_CB), w_spec(OFF_CC), w_spec(OFF_CX),
            pl.BlockSpec((3, tn), lambda i, j: (0, j)),
        ],
        out_specs=pl.BlockSpec((tm, tn), lambda i, j: (i, j)),
        out_shape=jax.ShapeDtypeStruct((T, D_MODEL), BF16),
        scratch_shapes=[pltpu.VMEM((tm + 2 * HALO, D_MODEL), BF16),
                        pltpu.VMEM((tm + 2 * HALO, tn), F32)],
        name="conv_proj",
        compiler_params=_cparams(("arbitrary", "arbitrary")),
    )(x, x, x, lng, lnb, w_in, w_in, w_in, conv_w)


def _attn_kernel(q_ref, k_ref, v_ref, o_ref, m_ref, l_ref, acc_ref, *, tq, tk, n_kv):
    q4 = jnp.concatenate([q_ref[:, g * HEAD_DIM:(g + 1) * HEAD_DIM] for g in range(GROUPS)], axis=0)
    m_ref[...] = jnp.full(m_ref.shape, -jnp.inf, F32)
    l_ref[...] = jnp.zeros(l_ref.shape, F32)
    acc_ref[...] = jnp.zeros(acc_ref.shape, F32)

    def body(c, carry):
        start = pl.multiple_of(c * tk, tk)
        kc = k_ref[pl.ds(start, tk), :]
        vc = v_ref[pl.ds(start, tk), :]
        s = lax.dot_general(q4, kc, (((1,), (1,)), ((), ())), preferred_element_type=F32)
        m_prev = m_ref[...]
        m_new = jnp.maximum(m_prev, jnp.max(s, axis=-1, keepdims=True))
        p = jnp.exp(s - m_new)
        alpha = jnp.exp(m_prev - m_new)
        l_ref[...] = alpha * l_ref[...] + jnp.sum(p, axis=-1, keepdims=True)
        acc_ref[...] = alpha * acc_ref[...] + jnp.dot(p.astype(BF16), vc, preferred_element_type=F32)
        m_ref[...] = m_new
        return carry

    lax.fori_loop(0, n_kv, body, 0)
    out = acc_ref[...] / l_ref[...]
    for g in range(GROUPS):
        o_ref[:, g * HEAD_DIM:(g + 1) * HEAD_DIM] = out[g * tq:(g + 1) * tq, :].astype(BF16)


def _attn_call(q, k, v, batch, seq):
    T = q.shape[0]
    tq = min(TQ_ATTN, seq)
    tk = min(TK_ATTN, seq)
    n_q = seq // tq
    gw = GROUPS * HEAD_DIM
    return pl.pallas_call(
        functools.partial(_attn_kernel, tq=tq, tk=tk, n_kv=seq // tk),
        grid=(batch, N_KV_HEADS, n_q),
        in_specs=[
            pl.BlockSpec((tq, gw), lambda b, h, i: (b * n_q + i, h)),
            pl.BlockSpec((seq, HEAD_DIM), lambda b, h, i: (b, h)),
            pl.BlockSpec((seq, HEAD_DIM), lambda b, h, i: (b, h)),
        ],
        out_specs=pl.BlockSpec((tq, gw), lambda b, h, i: (b * n_q + i, h)),
        out_shape=jax.ShapeDtypeStruct((T, ATTN_WIDTH), BF16),
        scratch_shapes=[pltpu.VMEM((GROUPS * tq, 1), F32),
                        pltpu.VMEM((GROUPS * tq, 1), F32),
                        pltpu.VMEM((GROUPS * tq, HEAD_DIM), F32)],
        name="gqa_attention",
        compiler_params=_cparams(("arbitrary", "arbitrary", "arbitrary")),
    )(q, k, v)


def _attn_bounded_kernel(nm_ref, q_ref, k_ref, v_ref, o_ref, ka_ref, va_ref, acc_ref, *, tq, tk, n_kv):
    @pl.when(pl.program_id(2) == 0)
    def _():
        lane = lax.broadcasted_iota(jnp.int32, k_ref.shape, 1)
        ka_ref[:, :HEAD_DIM] = k_ref[...]
        ka_ref[:, HEAD_DIM:] = jnp.where(lane == 0, 1.0, 0.0).astype(BF16)
        va_ref[:, :HEAD_DIM] = v_ref[...]
        va_ref[:, HEAD_DIM:] = jnp.ones(v_ref.shape, BF16)

    lane = lax.broadcasted_iota(jnp.int32, (tq, HEAD_DIM), 1)
    shift = jnp.where(lane == 0, nm_ref[0], 0.0).astype(BF16)
    q4 = jnp.concatenate(
        [jnp.concatenate([q_ref[:, g * HEAD_DIM:(g + 1) * HEAD_DIM], shift], axis=1) for g in range(GROUPS)],
        axis=0)
    acc_ref[...] = jnp.zeros(acc_ref.shape, F32)

    def body(c, carry):
        start = pl.multiple_of(c * tk, tk)
        s = lax.dot_general(q4, ka_ref[pl.ds(start, tk), :], (((1,), (1,)), ((), ())),
                            preferred_element_type=F32)
        p = jnp.exp(s).astype(BF16)
        acc_ref[...] += jnp.dot(p, va_ref[pl.ds(start, tk), :], preferred_element_type=F32)
        return carry

    lax.fori_loop(0, n_kv, body, 0, unroll=True)
    out = acc_ref[:, :HEAD_DIM] / acc_ref[:, HEAD_DIM:]
    for g in range(GROUPS):
        o_ref[:, g * HEAD_DIM:(g + 1) * HEAD_DIM] = out[g * tq:(g + 1) * tq, :].astype(BF16)


def _attn_bounded_call(neg_shift, q, k, v, batch, seq):
    T = q.shape[0]
    tq = min(TQ_ATTN, seq)
    tk = min(TK_ATTN_BOUNDED, seq)
    n_q = seq // tq
    gw = GROUPS * HEAD_DIM
    return pl.pallas_call(
        functools.partial(_attn_bounded_kernel, tq=tq, tk=tk, n_kv=seq // tk),
        grid=(batch, N_KV_HEADS, n_q),
        in_specs=[
            pl.BlockSpec(memory_space=pltpu.SMEM),
            pl.BlockSpec((tq, gw), lambda b, h, i: (b * n_q + i, h)),
            pl.BlockSpec((seq, HEAD_DIM), lambda b, h, i: (b, h)),
            pl.BlockSpec((seq, HEAD_DIM), lambda b, h, i: (b, h)),
        ],
        out_specs=pl.BlockSpec((tq, gw), lambda b, h, i: (b * n_q + i, h)),
        out_shape=jax.ShapeDtypeStruct((T, ATTN_WIDTH), BF16),
        scratch_shapes=[pltpu.VMEM((seq, 2 * HEAD_DIM), BF16),
                        pltpu.VMEM((seq, 2 * HEAD_DIM), BF16),
                        pltpu.VMEM((GROUPS * tq, 2 * HEAD_DIM), F32)],
        name="gqa_attention_bounded",
        compiler_params=_cparams(("arbitrary", "arbitrary", "arbitrary")),
    )(neg_shift, q, k, v)


MAX_SCORE_SHIFT = 40.0


def _score_bound(q_norm, k_norm):
    slack = 1.02
    return slack * HEAD_DIM * ATTN_SCALE * jnp.max(jnp.abs(q_norm)) * jnp.max(jnp.abs(k_norm))


def _attention(q, k, v, q_norm, k_norm, batch, seq):
    m = _score_bound(q_norm, k_norm)
    return lax.cond(
        m <= MAX_SCORE_SHIFT,
        lambda: _attn_bounded_call((-m).reshape(1).astype(F32), q, k, v, batch, seq),
        lambda: _attn_call(q, k, v, batch, seq))


def _mix_out_kernel(x_ref, lng_ref, lnb_ref, a_ref, c_ref, g_ref, wa_ref, wc_ref, wo_ref, g1_ref, b1_ref,
                    wrh_ref, wrl_ref, br_ref, cnt_in_ref, x1_ref, x1p_ref, idx_ref, gate_ref, rank_ref, cnt_out_ref,
                    cnt_ref):
    att = jnp.dot(a_ref[...], wa_ref[...], preferred_element_type=F32)
    cnv = jnp.dot(c_ref[...], wc_ref[...], preferred_element_type=F32)
    merged = (g_ref[:, :D_MODEL].astype(F32) * att + g_ref[:, D_MODEL:].astype(F32) * cnv).astype(BF16)
    out = jnp.dot(merged, wo_ref[...], preferred_element_type=F32)
    xn = _layer_norm(x_ref[...], lng_ref[...], lnb_ref[...])
    x1 = _layer_norm(DEEPNORM_ALPHA * xn + out, g1_ref[...], b1_ref[...])
    x1_ref[...] = x1
    x1p_ref[...] = _pack_bf16_pairs(x1)
    x_hi = x1.astype(BF16)
    x_lo = (x1 - x_hi.astype(F32)).astype(BF16)
    logits = (jnp.dot(x_hi, wrh_ref[...], preferred_element_type=F32)
              + jnp.dot(x_hi, wrl_ref[...], preferred_element_type=F32)
              + jnp.dot(x_lo, wrh_ref[...], preferred_element_type=F32)) + br_ref[...]
    lane = lax.broadcasted_iota(jnp.int32, logits.shape, 1).astype(F32)
    vals = logits
    tops, ids = [], []
    for _ in range(TOP_K):
        mx = jnp.max(vals, axis=-1, keepdims=True)
        ix = jnp.min(jnp.where(vals == mx, lane, float(N_EXPERTS)), axis=-1, keepdims=True)
        tops.append(mx)
        ids.append(ix)
        vals = jnp.where(lane == ix, -jnp.inf, vals)
    exps = [jnp.exp(t - tops[0]) for t in tops]
    den = exps[0] + exps[1] + exps[2] + exps[3]
    out_lane = lax.broadcasted_iota(jnp.int32, idx_ref.shape, 1)
    idx_out = jnp.zeros(idx_ref.shape, F32)
    gate_out = jnp.zeros(gate_ref.shape, F32)
    for kk in range(TOP_K):
        idx_out = jnp.where(out_lane == kk, ids[kk], idx_out)
        gate_out = jnp.where(out_lane == kk, exps[kk] / den, gate_out)
    idx_ref[...] = idx_out.astype(jnp.int32)
    gate_ref[...] = gate_out

    @pl.when(pl.program_id(0) == 0)
    def _():
        cnt_ref[...] = cnt_in_ref[...]

    tm = logits.shape[0]
    picks = [(lane == ids[kk]).astype(F32) for kk in range(TOP_K)]
    per_tok = picks[0] + picks[1] + picks[2] + picks[3]
    earlier = (lax.broadcasted_iota(jnp.int32, (tm, tm), 1)
               < lax.broadcasted_iota(jnp.int32, (tm, tm), 0)).astype(BF16)
    before = cnt_ref[...] + jnp.dot(earlier, per_tok.astype(BF16), preferred_element_type=F32)
    rank_out = jnp.zeros(rank_ref.shape, F32)
    for kk in range(TOP_K):
        rank_out = jnp.where(out_lane == kk, jnp.sum(picks[kk] * before, axis=-1, keepdims=True), rank_out)
    rank_ref[...] = rank_out.astype(jnp.int32)
    cnt_ref[...] += jnp.sum(per_tok, axis=0, keepdims=True)
    cnt_out_ref[...] = cnt_ref[...]


def _mix_out_call(x, lng, lnb, attn, conv_in, gates, wa, wc, wo, g1, b1, wr_hi, wr_lo, br, counts):
    T = x.shape[0]
    tm = TM_OUT
    row = lambda n: pl.BlockSpec((tm, n), lambda i: (i, 0))
    vec = lambda n: pl.BlockSpec((1, n), lambda i: (0, 0))
    resident = lambda r, c: pl.BlockSpec((r, c), lambda i: (0, 0), pipeline_mode=pl.Buffered(1))
    return pl.pallas_call(
        _mix_out_kernel,
        grid=(T // tm,),
        in_specs=[row(D_MODEL), vec(D_MODEL), vec(D_MODEL),
                  row(ATTN_WIDTH), row(D_MODEL), row(2 * D_MODEL),
                  resident(ATTN_WIDTH, D_MODEL), resident(D_MODEL, D_MODEL), resident(D_MODEL, D_MODEL),
                  vec(D_MODEL), vec(D_MODEL),
                  pl.BlockSpec((D_MODEL, N_EXPERTS), lambda i: (0, 0)),
                  pl.BlockSpec((D_MODEL, N_EXPERTS), lambda i: (0, 0)),
                  vec(N_EXPERTS), vec(N_EXPERTS)],
        out_specs=[row(D_MODEL), row(D_MODEL // 2), row(LANES), row(LANES), row(LANES), vec(N_EXPERTS)],
        out_shape=[jax.ShapeDtypeStruct((T, D_MODEL), F32),
                   jax.ShapeDtypeStruct((T, D_MODEL // 2), jnp.uint32),
                   jax.ShapeDtypeStruct((T, LANES), jnp.int32),
                   jax.ShapeDtypeStruct((T, LANES), F32),
                   jax.ShapeDtypeStruct((T, LANES), jnp.int32),
                   jax.ShapeDtypeStruct((1, N_EXPERTS), F32)],
        scratch_shapes=[pltpu.VMEM((1, N_EXPERTS), F32)],
        name="mix_out_router",
        compiler_params=_cparams(("arbitrary",)),
    )(x, lng, lnb, attn, conv_in, gates, wa, wc, wo, g1, b1, wr_hi, wr_lo, br, counts)


def _dispatch_copy(stage, par, r, dst_hbm, slot_row, sem):
    return pltpu.make_async_copy(stage.at[par, pl.ds(r, 1), :], dst_hbm.at[pl.ds(slot_row, 1), :], sem.at[par])


def _dispatch_wait(stage, dst_hbm, sem, par, n):
    def body(r, c):
        _dispatch_copy(stage, par, 0, dst_hbm, 0, sem).wait()
        return c
    lax.fori_loop(0, n, body, 0, unroll=8)


def _dispatch_kernel(dest_ref, x_ref, xs_in_hbm, xs_hbm, stage, sem, *, tm, n_tiles):
    del xs_in_hbm
    i = pl.program_id(0)
    par = i % 2
    n = TOP_K * tm

    @pl.when(i >= 2)
    def _():
        _dispatch_wait(stage, xs_hbm, sem, par, n)

    stage[par] = x_ref[...]
    for r in range(tm):
        for kk in range(TOP_K):
            _dispatch_copy(stage, par, r, xs_hbm, dest_ref[0, 0, r * TOP_K + kk], sem).start()

    @pl.when(i == n_tiles - 1)
    def _():
        _dispatch_wait(stage, xs_hbm, sem, par, n)

        @pl.when(i >= 1)
        def _():
            _dispatch_wait(stage, xs_hbm, sem, 1 - par, n)


def _dispatch_call(dest, x1p, xs):
    T, width = x1p.shape
    tm = TM_DISPATCH
    n_tiles = T // tm
    d3 = dest.reshape(n_tiles, 1, TOP_K * tm)
    return pl.pallas_call(
        functools.partial(_dispatch_kernel, tm=tm, n_tiles=n_tiles),
        grid=(n_tiles,),
        in_specs=[
            pl.BlockSpec((1, 1, TOP_K * tm), lambda i: (i, 0, 0), memory_space=pltpu.SMEM),
            pl.BlockSpec((tm, width), lambda i: (i, 0)),
            pl.BlockSpec(memory_space=pl.ANY),
        ],
        out_specs=pl.BlockSpec(memory_space=pl.ANY),
        out_shape=jax.ShapeDtypeStruct(xs.shape, xs.dtype),
        scratch_shapes=[pltpu.VMEM((2, tm, width), x1p.dtype), pltpu.SemaphoreType.DMA((2,))],
        input_output_aliases={2: 0},
        name="moe_dispatch",
        compiler_params=_cparams(("arbitrary",)),
    )(d3, x1p, xs)


def _expert_kernel(te_ref, nu_ref, xs_ref, wgu_ref, bgu_ref, wd_ref, bd_ref, o_ref, xb_ref, *, fc):
    i = pl.program_id(0)
    n_used = nu_ref[0]
    half = D_MODEL // 2

    @pl.when(i < n_used)
    def _():
        lo, hi = _unpack_bf16_pairs(xs_ref[...])
        xb_ref[:, :half] = lo
        xb_ref[:, half:] = hi
        xb = xb_ref[...]
        acc = jnp.broadcast_to(bd_ref[...], (xb.shape[0], D_MODEL))
        for c in range(D_EXPERT // fc):
            cg = slice(c * fc, (c + 1) * fc)
            cu = slice(D_EXPERT + c * fc, D_EXPERT + (c + 1) * fc)
            hg = jnp.dot(xb, wgu_ref[:, cg], preferred_element_type=F32) + bgu_ref[:, cg]
            hu = jnp.dot(xb, wgu_ref[:, cu], preferred_element_type=F32) + bgu_ref[:, cu]
            g = jnp.minimum(hg, SWIGLU_LIMIT)
            u = jnp.clip(hu, -SWIGLU_LIMIT, SWIGLU_LIMIT)
            act = (u + 1.0) * (g * jax.nn.sigmoid(SWIGLU_ALPHA * g))
            acc = acc + jnp.dot(act.astype(BF16), wd_ref[cg, :], preferred_element_type=F32)
        o_ref[...] = _pack_bf16_pairs(acc)

    @pl.when(i >= n_used)
    def _():
        o_ref[...] = jnp.zeros(o_ref.shape, o_ref.dtype)


def _expert_call(tile_e, n_used, xs, wgu, bgu, wd, bd):
    n_slots = xs.shape[0]
    tm = TM_MOE
    n_tiles = n_slots // tm

    def live(i, nu):
        return jnp.minimum(i, nu[0] - 1)

    once = pl.Buffered(1)
    grid_spec = pltpu.PrefetchScalarGridSpec(
        num_scalar_prefetch=2,
        grid=(n_tiles,),
        in_specs=[
            pl.BlockSpec((tm, D_MODEL // 2), lambda i, te, nu: (live(i, nu), 0)),
            pl.BlockSpec((None, D_MODEL, 2 * D_EXPERT), lambda i, te, nu: (te[i], 0, 0), pipeline_mode=once),
            pl.BlockSpec((None, 1, 2 * D_EXPERT), lambda i, te, nu: (te[i], 0, 0)),
            pl.BlockSpec((None, D_EXPERT, D_MODEL), lambda i, te, nu: (te[i], 0, 0)),
            pl.BlockSpec((None, 1, D_MODEL), lambda i, te, nu: (te[i], 0, 0)),
        ],
        out_specs=pl.BlockSpec((tm, D_MODEL // 2), lambda i, te, nu: (i, 0)),
        scratch_shapes=[pltpu.VMEM((tm, D_MODEL), BF16)],
    )
    return pl.pallas_call(
        functools.partial(_expert_kernel, fc=FC_MOE),
        grid_spec=grid_spec,
        out_shape=jax.ShapeDtypeStruct((n_slots, D_MODEL // 2), jnp.uint32),
        name="expert_ffn",
        compiler_params=_cparams(("arbitrary",)),
    )(tile_e, n_used, xs, wgu, bgu, wd, bd)


def _row_copy(src_hbm, row, buf, slot, r, sem):
    return pltpu.make_async_copy(src_hbm.at[pl.ds(row, 1), :], buf.at[slot, pl.ds(r, 1), :], sem.at[slot])


def _start_rows(idx_ref, src_hbm, buf, slot, sem, n_rows):
    for r in range(n_rows):
        _row_copy(src_hbm, idx_ref[0, 0, r], buf, slot, r, sem).start()


def _wait_rows(src_hbm, buf, slot, sem, n_rows):
    def body(r, c):
        _row_copy(src_hbm, 0, buf, slot, r, sem).wait()
        return c
    lax.fori_loop(0, n_rows, body, 0, unroll=8)


def _combine_kernel(d0_ref, dn_ref, e_hbm, x1_ref, gate_ref, g2_ref, b2_ref, y_ref, buf, sem, *, tm, n_tiles):
    i = pl.program_id(0)
    slot = i % 2
    n_rows = TOP_K * tm

    @pl.when(i == 0)
    def _():
        _start_rows(d0_ref, e_hbm, buf, 0, sem, n_rows)

    @pl.when(i + 1 < n_tiles)
    def _():
        _start_rows(dn_ref, e_hbm, buf, 1 - slot, sem, n_rows)

    _wait_rows(e_hbm, buf, slot, sem, n_rows)
    gate = gate_ref[...]
    ffn_lo = jnp.zeros((tm, D_MODEL // 2), F32)
    ffn_hi = jnp.zeros((tm, D_MODEL // 2), F32)
    for kk in range(TOP_K):
        lo, hi = _unpack_pairs_f32(buf[slot, kk * tm:(kk + 1) * tm, :])
        ffn_lo = ffn_lo + gate[:, kk:kk + 1] * lo
        ffn_hi = ffn_hi + gate[:, kk:kk + 1] * hi
    ffn = jnp.concatenate([ffn_lo, ffn_hi], axis=1)
    y_ref[...] = _layer_norm(DEEPNORM_ALPHA * x1_ref[...] + ffn, g2_ref[...], b2_ref[...])


def _combine_call(dest, eout, x1, gate_pad, g2, b2):
    T = x1.shape[0]
    tm = TM_COMB
    n_tiles = T // tm
    d3 = dest.reshape(n_tiles, tm, TOP_K).transpose(0, 2, 1).reshape(n_tiles, 1, TOP_K * tm)
    smem_blk = lambda imap: pl.BlockSpec((1, 1, TOP_K * tm), imap, memory_space=pltpu.SMEM)
    vec = pl.BlockSpec((1, D_MODEL), lambda i: (0, 0))
    return pl.pallas_call(
        functools.partial(_combine_kernel, tm=tm, n_tiles=n_tiles),
        grid=(n_tiles,),
        in_specs=[
            smem_blk(lambda i: (0, 0, 0)),
            smem_blk(lambda i: (jnp.minimum(i + 1, n_tiles - 1), 0, 0)),
            pl.BlockSpec(memory_space=pl.ANY),
            pl.BlockSpec((tm, D_MODEL), lambda i: (i, 0)),
            pl.BlockSpec((tm, LANES), lambda i: (i, 0)),
            vec, vec,
        ],
        out_specs=pl.BlockSpec((tm, D_MODEL), lambda i: (i, 0)),
        out_shape=jax.ShapeDtypeStruct((T, D_MODEL), F32),
        scratch_shapes=[pltpu.VMEM((2, TOP_K * tm, D_MODEL // 2), jnp.uint32),
                        pltpu.SemaphoreType.DMA((2,))],
        name="combine_ln",
        compiler_params=_cparams(("arbitrary",)),
    )(d3, d3, eout, x1, gate_pad, g2, b2)


def _rope_tables(seq):
    pos = jnp.arange(seq, dtype=jnp.int32)
    row = (pos // GRID_W).astype(F32)
    col = (pos % GRID_W).astype(F32)
    inv_freq = ROPE_THETA ** (-jnp.arange(N_FREQ, dtype=F32) / N_FREQ)
    ang_r = row[:, None] * inv_freq[None, :]
    ang_c = col[:, None] * inv_freq[None, :]
    zero = jnp.zeros_like(ang_r)
    cos = jnp.concatenate([jnp.cos(ang_r)] * 2 + [jnp.cos(ang_c)] * 2, axis=-1)
    sa = jnp.concatenate([-jnp.sin(ang_r), zero, -jnp.sin(ang_c), zero], axis=-1)
    sb = jnp.concatenate([zero, jnp.sin(ang_r), zero, jnp.sin(ang_c)], axis=-1)
    return cos, sa, sb


def _dispatch_plan(idx, rank, counts, tm):
    T = idx.shape[0]
    n_tiles = T * TOP_K // tm + N_EXPERTS
    padded = (counts + tm - 1) // tm * tm
    pend = jnp.cumsum(padded)
    pstart = pend - padded
    onehot = idx[:, :, None] == jnp.arange(N_EXPERTS, dtype=jnp.int32)[None, None, :]
    dest = jnp.sum(jnp.where(onehot, pstart[None, None, :], 0), axis=-1) + rank
    n_used = (pend[-1] // tm).astype(jnp.int32)
    tile_start = jnp.minimum(jnp.arange(n_tiles, dtype=jnp.int32), n_used - 1) * tm
    tile_e = jnp.minimum(jnp.searchsorted(pend, tile_start, side='right'), N_EXPERTS - 1).astype(jnp.int32)
    return tile_e, n_used.reshape(1), dest, n_tiles * tm


def _mixers(x3, p, counts):
    batch, seq, _ = x3.shape
    x = x3.reshape(batch * seq, D_MODEL)
    cos, sa, sb = _rope_tables(seq)
    q, k, v, gates = _qkvg_call(x, p["ln_in_g"], p["ln_in_b"], p["w_in"], cos, sa, sb,
                                p["q_norm"], p["k_norm"], seq)
    conv_in = _conv_call(x, p["ln_in_g"], p["ln_in_b"], p["w_in"], p["conv_w"], seq)
    attn = _attention(q, k, v, p["q_norm"], p["k_norm"], batch, seq)
    return _mix_out_call(x, p["ln_in_g"], p["ln_in_b"], attn, conv_in, gates,
                         p["w_attn_out"], p["w_conv_out"], p["w_o"],
                         p["ln1_g"], p["ln1_b"], p["w_router_hi"], p["w_router_lo"], p["b_router"], counts)


def _trunks(xs3, p):
    counts = jnp.zeros((1, N_EXPERTS), F32)
    mixed = []
    for x3 in xs3:
        mixed.append(_mixers(x3, p, counts))
        counts = mixed[-1][5]
    idx_all = jnp.concatenate([m[2][:, :TOP_K] for m in mixed], axis=0)
    rank_all = jnp.concatenate([m[4][:, :TOP_K] for m in mixed], axis=0)
    tile_e, n_used, dest_all, n_slots = _dispatch_plan(idx_all, rank_all, counts[0].astype(jnp.int32), TM_MOE)
    slots = jnp.zeros((n_slots, D_MODEL // 2), jnp.uint32)
    row0 = 0
    dests = []
    for m in mixed:
        n_tok = m[0].shape[0]
        dests.append(dest_all[row0:row0 + n_tok])
        slots = _dispatch_call(dests[-1], m[1], slots)
        row0 += n_tok
    eout = _expert_call(tile_e, n_used, slots, p["w_gate_up"], p["b_gate_up"], p["w_down"], p["b_down"])
    outs = []
    for x3, m, dest in zip(xs3, mixed, dests):
        y = _combine_call(dest, eout, m[0], m[3], p["ln2_g"], p["ln2_b"])
        outs.append(y.reshape(x3.shape))
    return tuple(outs)


def _prepare(ln_in_g, ln_in_b, w_in, q_norm, k_norm, conv_w, w_attn_out, w_conv_out, w_o,
             ln1_g, ln1_b, w_router, b_router, w_gate_up, b_gate_up, w_down, b_down, ln2_g, ln2_b):
    assert w_in.shape[0] == DEPTH
    vec = lambda a: a.reshape(1, -1)
    return {
        "ln_in_g": vec(ln_in_g), "ln_in_b": vec(ln_in_b),
        "w_in": w_in[0].astype(BF16),
        "q_norm": vec(q_norm[0]), "k_norm": vec(k_norm[0]),
        "conv_w": conv_w[0],
        "w_attn_out": w_attn_out[0].astype(BF16),
        "w_conv_out": w_conv_out[0].astype(BF16),
        "w_o": w_o[0].astype(BF16),
        "ln1_g": vec(ln1_g[0]), "ln1_b": vec(ln1_b[0]),
        "w_router_hi": w_router[0].astype(BF16),
        "w_router_lo": (w_router[0] - w_router[0].astype(BF16).astype(F32)).astype(BF16),
        "b_router": vec(b_router[0]),
        "w_gate_up": w_gate_up[0].astype(BF16),
        "b_gate_up": b_gate_up[0].reshape(N_EXPERTS, 1, 2 * D_EXPERT),
        "w_down": w_down[0].astype(BF16),
        "b_down": b_down[0].reshape(N_EXPERTS, 1, D_MODEL),
        "ln2_g": vec(ln2_g[0]), "ln2_b": vec(ln2_b[0]),
    }


def kernel(x_prompt, x_sample, ln_in_g, ln_in_b, w_in, q_norm, k_norm, conv_w, w_attn_out, w_conv_out, w_o,
           ln1_g, ln1_b, w_router, b_router, w_gate_up, b_gate_up, w_down, b_down, ln2_g, ln2_b):
    p = _prepare(ln_in_g, ln_in_b, w_in, q_norm, k_norm, conv_w, w_attn_out, w_conv_out, w_o,
                 ln1_g, ln1_b, w_router, b_router, w_gate_up, b_gate_up, w_down, b_down, ln2_g, ln2_b)
    return _trunks((x_prompt, x_sample), p)
```

```python
import functools

import jax
import jax.numpy as jnp
from jax import lax
from jax.experimental import pallas as pl
from jax.experimental.pallas import tpu as pltpu

F32 = jnp.float32
BF16 = jnp.bfloat16

D_MODEL = 2048
GRID_W = 64
N_HEADS = 16
N_KV_HEADS = 4
GROUPS = N_HEADS // N_KV_HEADS
HEAD_DIM = 128
ROT_HALF = HEAD_DIM // 2
N_FREQ = ROT_HALF // 2
ROPE_THETA = 10000.0
ATTN_WIDTH = N_HEADS * HEAD_DIM
KV_WIDTH = N_KV_HEADS * HEAD_DIM
N_EXPERTS = 32
TOP_K = 4
D_EXPERT = D_MODEL
SWIGLU_LIMIT = 7.0
SWIGLU_ALPHA = 1.702
RMS_EPS = 1e-6
LN_EPS = 1e-5
DEPTH = 1
DEEPNORM_ALPHA = (2 * DEPTH) ** 0.25
ATTN_SCALE = HEAD_DIM ** -0.5

OFF_K = ATTN_WIDTH
OFF_V = OFF_K + KV_WIDTH
OFF_CB = OFF_V + KV_WIDTH
OFF_CC = OFF_CB + D_MODEL
OFF_CX = OFF_CC + D_MODEL
OFF_GA = OFF_CX + D_MODEL
OFF_GC = OFF_GA + D_MODEL

LANES = 128
BF16_ROWS = 16
F32_ROWS = 8
VMEM_LIMIT = 56 * 1024 * 1024

TM_QKV = 1024
TM_PROJ = 512
TN_PROJ = 512
TQ_ATTN = 256
TQ_ATTN_BOUNDED = 512
TK_ATTN = 512
TK_ATTN_BOUNDED = 1024
TM_OUT = 256
TM_MOE = 512
FC_MOE = 512
TM_COMB = 128
TM_DISPATCH = 128
HALO = BF16_ROWS


def _cparams(sem):
    return pltpu.CompilerParams(dimension_semantics=sem, vmem_limit_bytes=VMEM_LIMIT)


def _pack_bf16_pairs(x):
    half = x.shape[1] // 2
    lo = lax.bitcast_convert_type(x[:, :half].astype(BF16).astype(F32), jnp.uint32)
    hi = lax.bitcast_convert_type(x[:, half:].astype(BF16).astype(F32), jnp.uint32)
    return (lo >> 16) | (hi & jnp.uint32(0xFFFF0000))


def _unpack_pairs_f32(w):
    lo = lax.bitcast_convert_type(w << 16, F32)
    hi = lax.bitcast_convert_type(w & jnp.uint32(0xFFFF0000), F32)
    return lo, hi


def _unpack_bf16_pairs(w):
    lo, hi = _unpack_pairs_f32(w)
    return lo.astype(BF16), hi.astype(BF16)


def _layer_norm(x, g, b):
    mu = jnp.mean(x, axis=-1, keepdims=True)
    xc = x - mu
    var = jnp.mean(xc * xc, axis=-1, keepdims=True)
    return xc * lax.rsqrt(var + LN_EPS) * g + b


N_QT = ATTN_WIDTH // TN_PROJ
J_K = N_QT
J_V = N_QT + 1
J_G = N_QT + 2
N_GT = 2 * D_MODEL // TN_PROJ
N_J = J_G + N_GT


def _qkvg_kernel(x_ref, lng_ref, lnb_ref, w_ref, cos_ref, sa_ref, sb_ref, qn_ref, kn_ref,
                 q_ref, k_ref, v_ref, g_ref, xn_ref, acc0_ref, acc1_ref):
    j = pl.program_id(1)
    accs = (acc0_ref, acc1_ref)

    def norm_rope(h, gain, scale):
        ms = jnp.mean(h * h, axis=-1, keepdims=True)
        hn = h * lax.rsqrt(ms + RMS_EPS) * gain
        r = (hn * cos_ref[...] + pltpu.roll(hn, HEAD_DIM - N_FREQ, 1) * sa_ref[...]
             + pltpu.roll(hn, N_FREQ, 1) * sb_ref[...])
        return r * scale

    def finish(jj, acc):
        if jj < J_K:
            for a in range(TN_PROJ // HEAD_DIM):
                sl = slice(a * HEAD_DIM, (a + 1) * HEAD_DIM)
                q_ref[:, sl] = norm_rope(acc[:, sl], qn_ref[...], ATTN_SCALE).astype(BF16)
        elif jj == J_K:
            for a in range(N_KV_HEADS):
                sl = slice(a * HEAD_DIM, (a + 1) * HEAD_DIM)
                k_ref[:, sl] = norm_rope(acc[:, sl], kn_ref[...], 1.0).astype(BF16)
        elif jj == J_V:
            for a in range(N_KV_HEADS):
                sl = slice(a * HEAD_DIM, (a + 1) * HEAD_DIM)
                v_ref[:, sl] = acc[:, sl].astype(BF16)
        else:
            for a in range(TN_PROJ // HEAD_DIM):
                sl = slice(a * HEAD_DIM, (a + 1) * HEAD_DIM)
                g_ref[:, sl] = jax.nn.sigmoid(acc[:, sl]).astype(BF16)

    for step in range(N_J + 1):
        @pl.when(j == step)
        def _(step=step):
            if step == 0:
                xn_ref[...] = _layer_norm(x_ref[...], lng_ref[...], lnb_ref[...]).astype(BF16)
            if step < N_J:
                accs[step % 2][...] = jnp.dot(xn_ref[...], w_ref[...], preferred_element_type=F32)
            if step >= 1:
                finish(step - 1, accs[(step - 1) % 2])


def _qkvg_call(x, lng, lnb, w_in, cos, sa, sb, qn, kn, seq):
    T = x.shape[0]
    tm, tn = TM_QKV, TN_PROJ
    n_seq = seq // tm
    gate_blk0 = OFF_GA // tn

    def w_map(i, j):
        jc = jnp.minimum(j, N_J - 1)
        return (0, jnp.where(jc < J_G, jc, jc - J_G + gate_blk0))

    rope_spec = pl.BlockSpec((tm, HEAD_DIM), lambda i, j: (i % n_seq, 0))
    vec_spec = lambda n: pl.BlockSpec((1, n), lambda i, j: (0, 0))
    return pl.pallas_call(
        _qkvg_kernel,
        grid=(T // tm, N_J + 1),
        in_specs=[
            pl.BlockSpec((tm, D_MODEL), lambda i, j: (i, 0)),
            vec_spec(D_MODEL), vec_spec(D_MODEL),
            pl.BlockSpec((D_MODEL, tn), w_map),
            rope_spec, rope_spec, rope_spec,
            vec_spec(HEAD_DIM), vec_spec(HEAD_DIM),
        ],
        out_specs=[
            pl.BlockSpec((tm, tn), lambda i, j: (i, jnp.clip(j - 1, 0, N_QT - 1))),
            pl.BlockSpec((tm, KV_WIDTH), lambda i, j: (i, 0)),
            pl.BlockSpec((tm, KV_WIDTH), lambda i, j: (i, 0)),
            pl.BlockSpec((tm, tn), lambda i, j: (i, jnp.clip(j - 1 - J_G, 0, N_GT - 1))),
            pl.BlockSpec((tm, D_MODEL), lambda i, j: (i, 0)),
        ],
        out_shape=[
            jax.ShapeDtypeStruct((T, ATTN_WIDTH), BF16),
            jax.ShapeDtypeStruct((T, KV_WIDTH), BF16),
            jax.ShapeDtypeStruct((T, KV_WIDTH), BF16),
            jax.ShapeDtypeStruct((T, 2 * D_MODEL), BF16),
            jax.ShapeDtypeStruct((T, D_MODEL), BF16),
        ],
        scratch_shapes=[pltpu.VMEM((tm, tn), F32), pltpu.VMEM((tm, tn), F32)],
        name="qkv_gate_proj",
        compiler_params=_cparams(("arbitrary", "arbitrary")),
    )(x, lng, lnb, w_in, cos, sa, sb, qn, kn)


W_SLAB = 1024


def _conv_kernel(xn_ref, xp_ref, xq_ref, *refs, n_seq, tm, tn):
    n_slab = D_MODEL // W_SLAB
    wcb_refs, wcc_refs, wcx_refs = refs[:n_slab], refs[n_slab:2 * n_slab], refs[2 * n_slab:3 * n_slab]
    cw_ref, o_ref, xe_ref, u_ref = refs[3 * n_slab:]
    i = pl.program_id(0)
    first = (i % n_seq) == 0
    last = (i % n_seq) == n_seq - 1
    xe_ref[0:HALO, :] = jnp.where(first, jnp.zeros_like(xp_ref[...]), xp_ref[...])
    xe_ref[HALO:HALO + tm, :] = xn_ref[...]
    xe_ref[HALO + tm:2 * HALO + tm, :] = jnp.where(last, jnp.zeros_like(xq_ref[...]), xq_ref[...])
    xe = xe_ref[...]
    for c in range(D_MODEL // tn):
        slab, ws = divmod(c * tn, W_SLAB)
        ws = slice(ws, ws + tn)
        cs = slice(c * tn, (c + 1) * tn)
        u_ref[...] = (jnp.dot(xe, wcc_refs[slab][:, ws], preferred_element_type=F32)
                      * jnp.dot(xe, wcx_refs[slab][:, ws], preferred_element_type=F32))
        cb = jnp.dot(xe_ref[HALO:HALO + tm, :], wcb_refs[slab][:, ws], preferred_element_type=F32)
        cw = cw_ref[:, cs]
        conv = (cw[0:1, :] * u_ref[HALO - 1:HALO - 1 + tm, :]
                + cw[1:2, :] * u_ref[HALO:HALO + tm, :]
                + cw[2:3, :] * u_ref[HALO + 1:HALO + 1 + tm, :])
        o_ref[:, cs] = (cb * conv).astype(BF16)


def _conv_call(xn, w_in, conv_w, seq):
    T = xn.shape[0]
    tm, tn = TM_PROJ, TN_PROJ
    n_seq = seq // tm
    halo_per_tile = tm // HALO
    n_halo_blocks = T // HALO
    n_slab = D_MODEL // W_SLAB

    def w_specs(off):
        return [pl.BlockSpec((D_MODEL, W_SLAB), lambda i, blk=off // W_SLAB + s: (0, blk),
                             pipeline_mode=pl.Buffered(1)) for s in range(n_slab)]

    return pl.pallas_call(
        functools.partial(_conv_kernel, n_seq=n_seq, tm=tm, tn=tn),
        grid=(T // tm,),
        in_specs=[
            pl.BlockSpec((tm, D_MODEL), lambda i: (i, 0)),
            pl.BlockSpec((HALO, D_MODEL), lambda i: (jnp.maximum(i * halo_per_tile - 1, 0), 0)),
            pl.BlockSpec((HALO, D_MODEL), lambda i: (jnp.minimum((i + 1) * halo_per_tile, n_halo_blocks - 1), 0)),
            *w_specs(OFF_CB), *w_specs(OFF_CC), *w_specs(OFF_CX),
            pl.BlockSpec((3, D_MODEL), lambda i: (0, 0)),
        ],
        out_specs=pl.BlockSpec((tm, D_MODEL), lambda i: (i, 0)),
        out_shape=jax.ShapeDtypeStruct((T, D_MODEL), BF16),
        scratch_shapes=[pltpu.VMEM((tm + 2 * HALO, D_MODEL), BF16),
                        pltpu.VMEM((tm + 2 * HALO, tn), F32)],
        name="conv_proj",
        compiler_params=_cparams(("arbitrary",)),
    )(xn, xn, xn, *([w_in] * (3 * n_slab)), conv_w)


def _attn_kernel(q_ref, k_ref, v_ref, o_ref, m_ref, l_ref, acc_ref, *, tq, tk, n_kv):
    q4 = jnp.concatenate([q_ref[:, g * HEAD_DIM:(g + 1) * HEAD_DIM] for g in range(GROUPS)], axis=0)
    m_ref[...] = jnp.full(m_ref.shape, -jnp.inf, F32)
    l_ref[...] = jnp.zeros(l_ref.shape, F32)
    acc_ref[...] = jnp.zeros(acc_ref.shape, F32)

    def body(c, carry):
        start = pl.multiple_of(c * tk, tk)
        kc = k_ref[pl.ds(start, tk), :]
        vc = v_ref[pl.ds(start, tk), :]
        s = lax.dot_general(q4, kc, (((1,), (1,)), ((), ())), preferred_element_type=F32)
        m_prev = m_ref[...]
        m_new = jnp.maximum(m_prev, jnp.max(s, axis=-1, keepdims=True))
        p = jnp.exp(s - m_new)
        alpha = jnp.exp(m_prev - m_new)
        l_ref[...] = alpha * l_ref[...] + jnp.sum(p, axis=-1, keepdims=True)
        acc_ref[...] = alpha * acc_ref[...] + jnp.dot(p.astype(BF16), vc, preferred_element_type=F32)
        m_ref[...] = m_new
        return carry

    lax.fori_loop(0, n_kv, body, 0)
    out = acc_ref[...] / l_ref[...]
    for g in range(GROUPS):
        o_ref[:, g * HEAD_DIM:(g + 1) * HEAD_DIM] = out[g * tq:(g + 1) * tq, :].astype(BF16)


def _attn_call(q, k, v, batch, seq):
    T = q.shape[0]
    tq = min(TQ_ATTN, seq)
    tk = min(TK_ATTN, seq)
    n_q = seq // tq
    gw = GROUPS * HEAD_DIM
    return pl.pallas_call(
        functools.partial(_attn_kernel, tq=tq, tk=tk, n_kv=seq // tk),
        grid=(batch, N_KV_HEADS, n_q),
        in_specs=[
            pl.BlockSpec((tq, gw), lambda b, h, i: (b * n_q + i, h)),
            pl.BlockSpec((seq, HEAD_DIM), lambda b, h, i: (b, h)),
            pl.BlockSpec((seq, HEAD_DIM), lambda b, h, i: (b, h)),
        ],
        out_specs=pl.BlockSpec((tq, gw), lambda b, h, i: (b * n_q + i, h)),
        out_shape=jax.ShapeDtypeStruct((T, ATTN_WIDTH), BF16),
        scratch_shapes=[pltpu.VMEM((GROUPS * tq, 1), F32),
                        pltpu.VMEM((GROUPS * tq, 1), F32),
                        pltpu.VMEM((GROUPS * tq, HEAD_DIM), F32)],
        name="gqa_attention",
        compiler_params=_cparams(("arbitrary", "arbitrary", "arbitrary")),
    )(q, k, v)


def _attn_bounded_kernel(nm_ref, q_ref, k_ref, v_ref, o_ref, ka_ref, va_ref, acc_ref, *, tq, tk, n_kv):
    @pl.when(pl.program_id(2) == 0)
    def _():
        lane = lax.broadcasted_iota(jnp.int32, k_ref.shape, 1)
        ka_ref[:, :HEAD_DIM] = k_ref[...]
        ka_ref[:, HEAD_DIM:] = jnp.where(lane == 0, 1.0, 0.0).astype(BF16)
        va_ref[:, :HEAD_DIM] = v_ref[...]
        va_ref[:, HEAD_DIM:] = jnp.ones(v_ref.shape, BF16)

    lane = lax.broadcasted_iota(jnp.int32, (tq, HEAD_DIM), 1)
    shift = jnp.where(lane == 0, nm_ref[0], 0.0).astype(BF16)
    q4 = jnp.concatenate(
        [jnp.concatenate([q_ref[:, g * HEAD_DIM:(g + 1) * HEAD_DIM], shift], axis=1) for g in range(GROUPS)],
        axis=0)
    acc_ref[...] = jnp.zeros(acc_ref.shape, F32)

    def body(c, carry):
        start = pl.multiple_of(c * tk, tk)
        s = lax.dot_general(q4, ka_ref[pl.ds(start, tk), :], (((1,), (1,)), ((), ())),
                            preferred_element_type=F32)
        p = jnp.exp(s).astype(BF16)
        acc_ref[...] += jnp.dot(p, va_ref[pl.ds(start, tk), :], preferred_element_type=F32)
        return carry

    lax.fori_loop(0, n_kv, body, 0, unroll=True)
    out = acc_ref[:, :HEAD_DIM] / acc_ref[:, HEAD_DIM:]
    for g in range(GROUPS):
        o_ref[:, g * HEAD_DIM:(g + 1) * HEAD_DIM] = out[g * tq:(g + 1) * tq, :].astype(BF16)


def _attn_bounded_call(neg_shift, q, k, v, batch, seq):
    T = q.shape[0]
    tq = min(TQ_ATTN_BOUNDED, seq)
    tk = min(TK_ATTN_BOUNDED, seq)
    n_q = seq // tq
    gw = GROUPS * HEAD_DIM
    return pl.pallas_call(
        functools.partial(_attn_bounded_kernel, tq=tq, tk=tk, n_kv=seq // tk),
        grid=(batch, N_KV_HEADS, n_q),
        in_specs=[
            pl.BlockSpec(memory_space=pltpu.SMEM),
            pl.BlockSpec((tq, gw), lambda b, h, i: (b * n_q + i, h)),
            pl.BlockSpec((seq, HEAD_DIM), lambda b, h, i: (b, h)),
            pl.BlockSpec((seq, HEAD_DIM), lambda b, h, i: (b, h)),
        ],
        out_specs=pl.BlockSpec((tq, gw), lambda b, h, i: (b * n_q + i, h)),
        out_shape=jax.ShapeDtypeStruct((T, ATTN_WIDTH), BF16),
        scratch_shapes=[pltpu.VMEM((seq, 2 * HEAD_DIM), BF16),
                        pltpu.VMEM((seq, 2 * HEAD_DIM), BF16),
                        pltpu.VMEM((GROUPS * tq, 2 * HEAD_DIM), F32)],
        name="gqa_attention_bounded",
        compiler_params=_cparams(("arbitrary", "arbitrary", "arbitrary")),
    )(neg_shift, q, k, v)


MAX_SCORE_SHIFT = 40.0


def _score_bound(q_norm, k_norm):
    slack = 1.02
    return slack * HEAD_DIM * ATTN_SCALE * jnp.max(jnp.abs(q_norm)) * jnp.max(jnp.abs(k_norm))


def _attention(q, k, v, q_norm, k_norm, batch, seq):
    m = _score_bound(q_norm, k_norm)
    return lax.cond(
        m <= MAX_SCORE_SHIFT,
        lambda: _attn_bounded_call((-m).reshape(1).astype(F32), q, k, v, batch, seq),
        lambda: _attn_call(q, k, v, batch, seq))


def _mix_out_kernel(x_ref, lng_ref, lnb_ref, a_ref, c_ref, g_ref, wa_ref, wc_ref, wo_ref, g1_ref, b1_ref,
                    wrh_ref, wrl_ref, br_ref, cnt_in_ref, x1_ref, x1p_ref, idx_ref, gate_ref, rank_ref, cnt_out_ref,
                    cnt_ref):
    att = jnp.dot(a_ref[...], wa_ref[...], preferred_element_type=F32)
    cnv = jnp.dot(c_ref[...], wc_ref[...], preferred_element_type=F32)
    merged = (g_ref[:, :D_MODEL].astype(F32) * att + g_ref[:, D_MODEL:].astype(F32) * cnv).astype(BF16)
    out = jnp.dot(merged, wo_ref[...], preferred_element_type=F32)
    xn = _layer_norm(x_ref[...], lng_ref[...], lnb_ref[...])
    x1 = _layer_norm(DEEPNORM_ALPHA * xn + out, g1_ref[...], b1_ref[...])
    x1_ref[...] = x1
    x1p_ref[...] = _pack_bf16_pairs(x1)
    x_hi = x1.astype(BF16)
    x_lo = (x1 - x_hi.astype(F32)).astype(BF16)
    logits = (jnp.dot(x_hi, wrh_ref[...], preferred_element_type=F32)
              + jnp.dot(x_hi, wrl_ref[...], preferred_element_type=F32)
              + jnp.dot(x_lo, wrh_ref[...], preferred_element_type=F32)) + br_ref[...]
    lane = lax.broadcasted_iota(jnp.int32, logits.shape, 1).astype(F32)
    vals = logits
    tops, ids = [], []
    for _ in range(TOP_K):
        mx = jnp.max(vals, axis=-1, keepdims=True)
        ix = jnp.min(jnp.where(vals == mx, lane, float(N_EXPERTS)), axis=-1, keepdims=True)
        tops.append(mx)
        ids.append(ix)
        vals = jnp.where(lane == ix, -jnp.inf, vals)
    exps = [jnp.exp(t - tops[0]) for t in tops]
    den = exps[0] + exps[1] + exps[2] + exps[3]
    out_lane = lax.broadcasted_iota(jnp.int32, idx_ref.shape, 1)
    idx_out = jnp.zeros(idx_ref.shape, F32)
    gate_out = jnp.zeros(gate_ref.shape, F32)
    for kk in range(TOP_K):
        idx_out = jnp.where(out_lane == kk, ids[kk], idx_out)
        gate_out = jnp.where(out_lane == kk, exps[kk] / den, gate_out)
    idx_ref[...] = idx_out.astype(jnp.int32)
    gate_ref[...] = gate_out

    @pl.when(pl.program_id(0) == 0)
    def _():
        cnt_ref[...] = cnt_in_ref[...]

    tm = logits.shape[0]
    picks = [(lane == ids[kk]).astype(F32) for kk in range(TOP_K)]
    per_tok = picks[0] + picks[1] + picks[2] + picks[3]
    earlier = (lax.broadcasted_iota(jnp.int32, (tm, tm), 1)
               < lax.broadcasted_iota(jnp.int32, (tm, tm), 0)).astype(BF16)
    before = cnt_ref[...] + jnp.dot(earlier, per_tok.astype(BF16), preferred_element_type=F32)
    rank_out = jnp.zeros(rank_ref.shape, F32)
    for kk in range(TOP_K):
        rank_out = jnp.where(out_lane == kk, jnp.sum(picks[kk] * before, axis=-1, keepdims=True), rank_out)
    rank_ref[...] = rank_out.astype(jnp.int32)
    cnt_ref[...] += jnp.sum(per_tok, axis=0, keepdims=True)
    cnt_out_ref[...] = cnt_ref[...]


def _mix_out_call(x, lng, lnb, attn, conv_in, gates, wa, wc, wo, g1, b1, wr_hi, wr_lo, br, counts):
    T = x.shape[0]
    tm = TM_OUT
    row = lambda n: pl.BlockSpec((tm, n), lambda i: (i, 0))
    vec = lambda n: pl.BlockSpec((1, n), lambda i: (0, 0))
    resident = lambda r, c: pl.BlockSpec((r, c), lambda i: (0, 0), pipeline_mode=pl.Buffered(1))
    return pl.pallas_call(
        _mix_out_kernel,
        grid=(T // tm,),
        in_specs=[row(D_MODEL), vec(D_MODEL), vec(D_MODEL),
                  row(ATTN_WIDTH), row(D_MODEL), row(2 * D_MODEL),
                  resident(ATTN_WIDTH, D_MODEL), resident(D_MODEL, D_MODEL), resident(D_MODEL, D_MODEL),
                  vec(D_MODEL), vec(D_MODEL),
                  pl.BlockSpec((D_MODEL, N_EXPERTS), lambda i: (0, 0)),
                  pl.BlockSpec((D_MODEL, N_EXPERTS), lambda i: (0, 0)),
                  vec(N_EXPERTS), vec(N_EXPERTS)],
        out_specs=[row(D_MODEL), row(D_MODEL // 2), row(LANES), row(LANES), row(LANES), vec(N_EXPERTS)],
        out_shape=[jax.ShapeDtypeStruct((T, D_MODEL), F32),
                   jax.ShapeDtypeStruct((T, D_MODEL // 2), jnp.uint32),
                   jax.ShapeDtypeStruct((T, LANES), jnp.int32),
                   jax.ShapeDtypeStruct((T, LANES), F32),
                   jax.ShapeDtypeStruct((T, LANES), jnp.int32),
                   jax.ShapeDtypeStruct((1, N_EXPERTS), F32)],
        scratch_shapes=[pltpu.VMEM((1, N_EXPERTS), F32)],
        name="mix_out_router",
        compiler_params=_cparams(("arbitrary",)),
    )(x, lng, lnb, attn, conv_in, gates, wa, wc, wo, g1, b1, wr_hi, wr_lo, br, counts)


def _dispatch_copy(stage, par, r, dst_hbm, slot_row, sem):
    return pltpu.make_async_copy(stage.at[par, pl.ds(r, 1), :], dst_hbm.at[pl.ds(slot_row, 1), :], sem.at[par])


def _dispatch_wait(stage, dst_hbm, sem, par, n):
    def body(r, c):
        _dispatch_copy(stage, par, 0, dst_hbm, 0, sem).wait()
        return c
    lax.fori_loop(0, n, body, 0, unroll=8)


def _dispatch_kernel(dest_ref, x_ref, xs_in_hbm, xs_hbm, stage, sem, *, tm, n_tiles):
    del xs_in_hbm
    i = pl.program_id(0)
    par = i % 2
    n = TOP_K * tm

    @pl.when(i >= 2)
    def _():
        _dispatch_wait(stage, xs_hbm, sem, par, n)

    stage[par] = x_ref[...]
    for r in range(tm):
        for kk in range(TOP_K):
            _dispatch_copy(stage, par, r, xs_hbm, dest_ref[0, 0, r * TOP_K + kk], sem).start()

    @pl.when(i == n_tiles - 1)
    def _():
        _dispatch_wait(stage, xs_hbm, sem, par, n)

        @pl.when(i >= 1)
        def _():
            _dispatch_wait(stage, xs_hbm, sem, 1 - par, n)


def _dispatch_call(dest, x1p, xs):
    T, width = x1p.shape
    tm = TM_DISPATCH
    n_tiles = T // tm
    d3 = dest.reshape(n_tiles, 1, TOP_K * tm)
    return pl.pallas_call(
        functools.partial(_dispatch_kernel, tm=tm, n_tiles=n_tiles),
        grid=(n_tiles,),
        in_specs=[
            pl.BlockSpec((1, 1, TOP_K * tm), lambda i: (i, 0, 0), memory_space=pltpu.SMEM),
            pl.BlockSpec((tm, width), lambda i: (i, 0)),
            pl.BlockSpec(memory_space=pl.ANY),
        ],
        out_specs=pl.BlockSpec(memory_space=pl.ANY),
        out_shape=jax.ShapeDtypeStruct(xs.shape, xs.dtype),
        scratch_shapes=[pltpu.VMEM((2, tm, width), x1p.dtype), pltpu.SemaphoreType.DMA((2,))],
        input_output_aliases={2: 0},
        name="moe_dispatch",
        compiler_params=_cparams(("arbitrary",)),
    )(d3, x1p, xs)


def _expert_kernel(te_ref, nu_ref, xs_ref, wgu_ref, bgu_ref, wd_ref, bd_ref, o_ref, xb_ref, *, fc):
    i = pl.program_id(0)
    n_used = nu_ref[0]
    half = D_MODEL // 2

    @pl.when(i < n_used)
    def _():
        lo, hi = _unpack_bf16_pairs(xs_ref[...])
        xb_ref[:, :half] = lo
        xb_ref[:, half:] = hi
        xb = xb_ref[...]
        acc = jnp.broadcast_to(bd_ref[...], (xb.shape[0], D_MODEL))
        for c in range(D_EXPERT // fc):
            cg = slice(c * fc, (c + 1) * fc)
            cu = slice(D_EXPERT + c * fc, D_EXPERT + (c + 1) * fc)
            hg = jnp.dot(xb, wgu_ref[:, cg], preferred_element_type=F32) + bgu_ref[:, cg]
            hu = jnp.dot(xb, wgu_ref[:, cu], preferred_element_type=F32) + bgu_ref[:, cu]
            g = jnp.minimum(hg, SWIGLU_LIMIT)
            u = jnp.clip(hu, -SWIGLU_LIMIT, SWIGLU_LIMIT)
            act = (u + 1.0) * (g * jax.nn.sigmoid(SWIGLU_ALPHA * g))
            acc = acc + jnp.dot(act.astype(BF16), wd_ref[cg, :], preferred_element_type=F32)
        o_ref[...] = _pack_bf16_pairs(acc)

    @pl.when(i >= n_used)
    def _():
        o_ref[...] = jnp.zeros(o_ref.shape, o_ref.dtype)


def _expert_call(tile_e, n_used, xs, wgu, bgu, wd, bd):
    n_slots = xs.shape[0]
    tm = TM_MOE
    n_tiles = n_slots // tm

    def live(i, nu):
        return jnp.minimum(i, nu[0] - 1)

    once = pl.Buffered(1)
    grid_spec = pltpu.PrefetchScalarGridSpec(
        num_scalar_prefetch=2,
        grid=(n_tiles,),
        in_specs=[
            pl.BlockSpec((tm, D_MODEL // 2), lambda i, te, nu: (live(i, nu), 0)),
            pl.BlockSpec((None, D_MODEL, 2 * D_EXPERT), lambda i, te, nu: (te[i], 0, 0), pipeline_mode=once),
            pl.BlockSpec((None, 1, 2 * D_EXPERT), lambda i, te, nu: (te[i], 0, 0)),
            pl.BlockSpec((None, D_EXPERT, D_MODEL), lambda i, te, nu: (te[i], 0, 0)),
            pl.BlockSpec((None, 1, D_MODEL), lambda i, te, nu: (te[i], 0, 0)),
        ],
        out_specs=pl.BlockSpec((tm, D_MODEL // 2), lambda i, te, nu: (i, 0)),
        scratch_shapes=[pltpu.VMEM((tm, D_MODEL), BF16)],
    )
    return pl.pallas_call(
        functools.partial(_expert_kernel, fc=FC_MOE),
        grid_spec=grid_spec,
        out_shape=jax.ShapeDtypeStruct((n_slots, D_MODEL // 2), jnp.uint32),
        name="expert_ffn",
        compiler_params=_cparams(("arbitrary",)),
    )(tile_e, n_used, xs, wgu, bgu, wd, bd)


def _row_copy(src_hbm, row, buf, slot, r, sem):
    return pltpu.make_async_copy(src_hbm.at[pl.ds(row, 1), :], buf.at[slot, pl.ds(r, 1), :], sem.at[slot])


def _start_rows(idx_ref, src_hbm, buf, slot, sem, n_rows):
    for r in range(n_rows):
        _row_copy(src_hbm, idx_ref[0, 0, r], buf, slot, r, sem).start()


def _wait_rows(src_hbm, buf, slot, sem, n_rows):
    def body(r, c):
        _row_copy(src_hbm, 0, buf, slot, r, sem).wait()
        return c
    lax.fori_loop(0, n_rows, body, 0, unroll=8)


def _combine_kernel(d0_ref, dn_ref, e_hbm, x1_ref, gate_ref, g2_ref, b2_ref, y_ref, buf, sem, *, tm, n_tiles):
    i = pl.program_id(0)
    slot = i % 2
    n_rows = TOP_K * tm

    @pl.when(i == 0)
    def _():
        _start_rows(d0_ref, e_hbm, buf, 0, sem, n_rows)

    @pl.when(i + 1 < n_tiles)
    def _():
        _start_rows(dn_ref, e_hbm, buf, 1 - slot, sem, n_rows)

    _wait_rows(e_hbm, buf, slot, sem, n_rows)
    gate = gate_ref[...]
    ffn_lo = jnp.zeros((tm, D_MODEL // 2), F32)
    ffn_hi = jnp.zeros((tm, D_MODEL // 2), F32)
    for kk in range(TOP_K):
        lo, hi = _unpack_pairs_f32(buf[slot, kk * tm:(kk + 1) * tm, :])
        ffn_lo = ffn_lo + gate[:, kk:kk + 1] * lo
        ffn_hi = ffn_hi + gate[:, kk:kk + 1] * hi
    ffn = jnp.concatenate([ffn_lo, ffn_hi], axis=1)
    y_ref[...] = _layer_norm(DEEPNORM_ALPHA * x1_ref[...] + ffn, g2_ref[...], b2_ref[...])


def _combine_call(dest, eout, x1, gate_pad, g2, b2):
    T = x1.shape[0]
    tm = TM_COMB
    n_tiles = T // tm
    d3 = dest.reshape(n_tiles, tm, TOP_K).transpose(0, 2, 1).reshape(n_tiles, 1, TOP_K * tm)
    smem_blk = lambda imap: pl.BlockSpec((1, 1, TOP_K * tm), imap, memory_space=pltpu.SMEM)
    vec = pl.BlockSpec((1, D_MODEL), lambda i: (0, 0))
    return pl.pallas_call(
        functools.partial(_combine_kernel, tm=tm, n_tiles=n_tiles),
        grid=(n_tiles,),
        in_specs=[
            smem_blk(lambda i: (0, 0, 0)),
            smem_blk(lambda i: (jnp.minimum(i + 1, n_tiles - 1), 0, 0)),
            pl.BlockSpec(memory_space=pl.ANY),
            pl.BlockSpec((tm, D_MODEL), lambda i: (i, 0)),
            pl.BlockSpec((tm, LANES), lambda i: (i, 0)),
            vec, vec,
        ],
        out_specs=pl.BlockSpec((tm, D_MODEL), lambda i: (i, 0)),
        out_shape=jax.ShapeDtypeStruct((T, D_MODEL), F32),
        scratch_shapes=[pltpu.VMEM((2, TOP_K * tm, D_MODEL // 2), jnp.uint32),
                        pltpu.SemaphoreType.DMA((2,))],
        name="combine_ln",
        compiler_params=_cparams(("arbitrary",)),
    )(d3, d3, eout, x1, gate_pad, g2, b2)


def _rope_tables(seq):
    pos = jnp.arange(seq, dtype=jnp.int32)
    row = (pos // GRID_W).astype(F32)
    col = (pos % GRID_W).astype(F32)
    inv_freq = ROPE_THETA ** (-jnp.arange(N_FREQ, dtype=F32) / N_FREQ)
    ang_r = row[:, None] * inv_freq[None, :]
    ang_c = col[:, None] * inv_freq[None, :]
    zero = jnp.zeros_like(ang_r)
    cos = jnp.concatenate([jnp.cos(ang_r)] * 2 + [jnp.cos(ang_c)] * 2, axis=-1)
    sa = jnp.concatenate([-jnp.sin(ang_r), zero, -jnp.sin(ang_c), zero], axis=-1)
    sb = jnp.concatenate([zero, jnp.sin(ang_r), zero, jnp.sin(ang_c)], axis=-1)
    return cos, sa, sb


def _dispatch_plan(idx, rank, counts, tm):
    T = idx.shape[0]
    n_tiles = T * TOP_K // tm + N_EXPERTS
    padded = (counts + tm - 1) // tm * tm
    pend = jnp.cumsum(padded)
    pstart = pend - padded
    onehot = idx[:, :, None] == jnp.arange(N_EXPERTS, dtype=jnp.int32)[None, None, :]
    dest = jnp.sum(jnp.where(onehot, pstart[None, None, :], 0), axis=-1) + rank
    n_used = (pend[-1] // tm).astype(jnp.int32)
    tile_start = jnp.minimum(jnp.arange(n_tiles, dtype=jnp.int32), n_used - 1) * tm
    tile_e = jnp.minimum(jnp.sum((tile_start[:, None] >= pend[None, :]).astype(jnp.int32), axis=1), N_EXPERTS - 1)
    return tile_e, n_used.reshape(1), dest, n_tiles * tm


def _mixers(x3, p, counts):
    batch, seq, _ = x3.shape
    x = x3.reshape(batch * seq, D_MODEL)
    cos, sa, sb = _rope_tables(seq)
    q, k, v, gates, xn = _qkvg_call(x, p["ln_in_g"], p["ln_in_b"], p["w_in"], cos, sa, sb,
                                    p["q_norm"], p["k_norm"], seq)
    conv_in = _conv_call(xn, p["w_in"], p["conv_w"], seq)
    attn = _attention(q, k, v, p["q_norm"], p["k_norm"], batch, seq)
    return _mix_out_call(x, p["ln_in_g"], p["ln_in_b"], attn, conv_in, gates,
                         p["w_attn_out"], p["w_conv_out"], p["w_o"],
                         p["ln1_g"], p["ln1_b"], p["w_router_hi"], p["w_router_lo"], p["b_router"], counts)


def _trunks(xs3, p):
    counts = jnp.zeros((1, N_EXPERTS), F32)
    mixed = []
    for x3 in xs3:
        mixed.append(_mixers(x3, p, counts))
        counts = mixed[-1][5]
    idx_all = jnp.concatenate([m[2][:, :TOP_K] for m in mixed], axis=0)
    rank_all = jnp.concatenate([m[4][:, :TOP_K] for m in mixed], axis=0)
    tile_e, n_used, dest_all, n_slots = _dispatch_plan(idx_all, rank_all, counts[0].astype(jnp.int32), TM_MOE)
    slots = jnp.zeros((n_slots, D_MODEL // 2), jnp.uint32)
    row0 = 0
    dests = []
    for m in mixed:
        n_tok = m[0].shape[0]
        dests.append(dest_all[row0:row0 + n_tok])
        slots = _dispatch_call(dests[-1], m[1], slots)
        row0 += n_tok
    eout = _expert_call(tile_e, n_used, slots, p["w_gate_up"], p["b_gate_up"], p["w_down"], p["b_down"])
    outs = []
    for x3, m, dest in zip(xs3, mixed, dests):
        y = _combine_call(dest, eout, m[0], m[3], p["ln2_g"], p["ln2_b"])
        outs.append(y.reshape(x3.shape))
    return tuple(outs)


def _prepare(ln_in_g, ln_in_b, w_in, q_norm, k_norm, conv_w, w_attn_out, w_conv_out, w_o,
             ln1_g, ln1_b, w_router, b_router, w_gate_up, b_gate_up, w_down, b_down, ln2_g, ln2_b):
    assert w_in.shape[0] == DEPTH
    vec = lambda a: a.reshape(1, -1)
    return {
        "ln_in_g": vec(ln_in_g), "ln_in_b": vec(ln_in_b),
        "w_in": w_in[0].astype(BF16),
        "q_norm": vec(q_norm[0]), "k_norm": vec(k_norm[0]),
        "conv_w": conv_w[0],
        "w_attn_out": w_attn_out[0].astype(BF16),
        "w_conv_out": w_conv_out[0].astype(BF16),
        "w_o": w_o[0].astype(BF16),
        "ln1_g": vec(ln1_g[0]), "ln1_b": vec(ln1_b[0]),
        "w_router_hi": w_router[0].astype(BF16),
        "w_router_lo": (w_router[0] - w_router[0].astype(BF16).astype(F32)).astype(BF16),
        "b_router": vec(b_router[0]),
        "w_gate_up": w_gate_up[0].astype(BF16),
        "b_gate_up": b_gate_up[0].reshape(N_EXPERTS, 1, 2 * D_EXPERT),
        "w_down": w_down[0].astype(BF16),
        "b_down": b_down[0].reshape(N_EXPERTS, 1, D_MODEL),
        "ln2_g": vec(ln2_g[0]), "ln2_b": vec(ln2_b[0]),
    }


def kernel(x_prompt, x_sample, ln_in_g, ln_in_b, w_in, q_norm, k_norm, conv_w, w_attn_out, w_conv_out, w_o,
           ln1_g, ln1_b, w_router, b_router, w_gate_up, b_gate_up, w_down, b_down, ln2_g, ln2_b):
    p = _prepare(ln_in_g, ln_in_b, w_in, q_norm, k_norm, conv_w, w_attn_out, w_conv_out, w_o,
                 ln1_g, ln1_b, w_router, b_router, w_gate_up, b_gate_up, w_down, b_down, ln2_g, ln2_b)
    return _trunks((x_prompt, x_sample), p)
```

```python
import functools

import jax
import jax.numpy as jnp
from jax import lax
from jax.experimental import pallas as pl
from jax.experimental.pallas import tpu as pltpu

F32 = jnp.float32
BF16 = jnp.bfloat16

D_MODEL = 2048
GRID_W = 64
N_HEADS = 16
N_KV_HEADS = 4
GROUPS = N_HEADS // N_KV_HEADS
HEAD_DIM = 128
ROT_HALF = HEAD_DIM // 2
N_FREQ = ROT_HALF // 2
ROPE_THETA = 10000.0
ATTN_WIDTH = N_HEADS * HEAD_DIM
KV_WIDTH = N_KV_HEADS * HEAD_DIM
N_EXPERTS = 32
TOP_K = 4
D_EXPERT = D_MODEL
SWIGLU_LIMIT = 7.0
SWIGLU_ALPHA = 1.702
RMS_EPS = 1e-6
LN_EPS = 1e-5
DEPTH = 1
DEEPNORM_ALPHA = (2 * DEPTH) ** 0.25
ATTN_SCALE = HEAD_DIM ** -0.5

OFF_K = ATTN_WIDTH
OFF_V = OFF_K + KV_WIDTH
OFF_CB = OFF_V + KV_WIDTH
OFF_CC = OFF_CB + D_MODEL
OFF_CX = OFF_CC + D_MODEL
OFF_GA = OFF_CX + D_MODEL
OFF_GC = OFF_GA + D_MODEL

LANES = 128
BF16_ROWS = 16
F32_ROWS = 8
VMEM_LIMIT = 56 * 1024 * 1024

TM_QKV = 1024
TM_PROJ = 512
TN_PROJ = 512
TQ_ATTN = 256
TQ_ATTN_BOUNDED = 512
TK_ATTN = 512
TK_ATTN_BOUNDED = 1024
TM_OUT = 256
TM_MOE = 512
FC_MOE = 512
TM_COMB = 128
TM_DISPATCH = 128
HALO = BF16_ROWS


def _cparams(sem):
    return pltpu.CompilerParams(dimension_semantics=sem, vmem_limit_bytes=VMEM_LIMIT)


def _pack_bf16_pairs(x):
    half = x.shape[1] // 2
    lo = lax.bitcast_convert_type(x[:, :half].astype(BF16).astype(F32), jnp.uint32)
    hi = lax.bitcast_convert_type(x[:, half:].astype(BF16).astype(F32), jnp.uint32)
    return (lo >> 16) | (hi & jnp.uint32(0xFFFF0000))


def _unpack_pairs_f32(w):
    lo = lax.bitcast_convert_type(w << 16, F32)
    hi = lax.bitcast_convert_type(w & jnp.uint32(0xFFFF0000), F32)
    return lo, hi


def _unpack_bf16_pairs(w):
    lo, hi = _unpack_pairs_f32(w)
    return lo.astype(BF16), hi.astype(BF16)


def _layer_norm(x, g, b):
    mu = jnp.mean(x, axis=-1, keepdims=True)
    xc = x - mu
    var = jnp.mean(xc * xc, axis=-1, keepdims=True)
    return xc * lax.rsqrt(var + LN_EPS) * g + b


N_QT = ATTN_WIDTH // TN_PROJ
J_K = N_QT
J_V = N_QT + 1
J_G = N_QT + 2
N_GT = 2 * D_MODEL // TN_PROJ
N_J = J_G + N_GT


def _qkvg_kernel(x_ref, lng_ref, lnb_ref, w_ref, cos_ref, sa_ref, sb_ref, qn_ref, kn_ref,
                 q_ref, k_ref, v_ref, g_ref, xn_ref, acc0_ref, acc1_ref):
    j = pl.program_id(1)
    accs = (acc0_ref, acc1_ref)

    def norm_rope(h, gain, scale):
        ms = jnp.mean(h * h, axis=-1, keepdims=True)
        hn = h * lax.rsqrt(ms + RMS_EPS) * gain
        r = (hn * cos_ref[...] + pltpu.roll(hn, HEAD_DIM - N_FREQ, 1) * sa_ref[...]
             + pltpu.roll(hn, N_FREQ, 1) * sb_ref[...])
        return r * scale

    def finish(jj, acc):
        if jj < J_K:
            for a in range(TN_PROJ // HEAD_DIM):
                sl = slice(a * HEAD_DIM, (a + 1) * HEAD_DIM)
                q_ref[:, sl] = norm_rope(acc[:, sl], qn_ref[...], ATTN_SCALE).astype(BF16)
        elif jj == J_K:
            for a in range(N_KV_HEADS):
                sl = slice(a * HEAD_DIM, (a + 1) * HEAD_DIM)
                k_ref[:, sl] = norm_rope(acc[:, sl], kn_ref[...], 1.0).astype(BF16)
        elif jj == J_V:
            for a in range(N_KV_HEADS):
                sl = slice(a * HEAD_DIM, (a + 1) * HEAD_DIM)
                v_ref[:, sl] = acc[:, sl].astype(BF16)
        else:
            for a in range(TN_PROJ // HEAD_DIM):
                sl = slice(a * HEAD_DIM, (a + 1) * HEAD_DIM)
                g_ref[:, sl] = jax.nn.sigmoid(acc[:, sl]).astype(BF16)

    for step in range(N_J + 1):
        @pl.when(j == step)
        def _(step=step):
            if step == 0:
                xn_ref[...] = _layer_norm(x_ref[...], lng_ref[...], lnb_ref[...]).astype(BF16)
            if step < N_J:
                accs[step % 2][...] = jnp.dot(xn_ref[...], w_ref[...], preferred_element_type=F32)
            if step >= 1:
                finish(step - 1, accs[(step - 1) % 2])


def _qkvg_call(x, lng, lnb, w_in, cos, sa, sb, qn, kn, seq):
    T = x.shape[0]
    tm, tn = TM_QKV, TN_PROJ
    n_seq = seq // tm
    gate_blk0 = OFF_GA // tn

    def w_map(i, j):
        jc = jnp.minimum(j, N_J - 1)
        return (0, jnp.where(jc < J_G, jc, jc - J_G + gate_blk0))

    rope_spec = pl.BlockSpec((tm, HEAD_DIM), lambda i, j: (i % n_seq, 0))
    vec_spec = lambda n: pl.BlockSpec((1, n), lambda i, j: (0, 0))
    return pl.pallas_call(
        _qkvg_kernel,
        grid=(T // tm, N_J + 1),
        in_specs=[
            pl.BlockSpec((tm, D_MODEL), lambda i, j: (i, 0)),
            vec_spec(D_MODEL), vec_spec(D_MODEL),
            pl.BlockSpec((D_MODEL, tn), w_map),
            rope_spec, rope_spec, rope_spec,
            vec_spec(HEAD_DIM), vec_spec(HEAD_DIM),
        ],
        out_specs=[
            pl.BlockSpec((tm, tn), lambda i, j: (i, jnp.clip(j - 1, 0, N_QT - 1))),
            pl.BlockSpec((tm, KV_WIDTH), lambda i, j: (i, 0)),
            pl.BlockSpec((tm, KV_WIDTH), lambda i, j: (i, 0)),
            pl.BlockSpec((tm, tn), lambda i, j: (i, jnp.clip(j - 1 - J_G, 0, N_GT - 1))),
            pl.BlockSpec((tm, D_MODEL), lambda i, j: (i, 0)),
        ],
        out_shape=[
            jax.ShapeDtypeStruct((T, ATTN_WIDTH), BF16),
            jax.ShapeDtypeStruct((T, KV_WIDTH), BF16),
            jax.ShapeDtypeStruct((T, KV_WIDTH), BF16),
            jax.ShapeDtypeStruct((T, 2 * D_MODEL), BF16),
            jax.ShapeDtypeStruct((T, D_MODEL), BF16),
        ],
        scratch_shapes=[pltpu.VMEM((tm, tn), F32), pltpu.VMEM((tm, tn), F32)],
        name="qkv_gate_proj",
        compiler_params=_cparams(("arbitrary", "arbitrary")),
    )(x, lng, lnb, w_in, cos, sa, sb, qn, kn)


W_SLAB = 1024


def _conv_kernel(xn_ref, xp_ref, xq_ref, *refs, n_seq, tm, tn):
    n_slab = D_MODEL // W_SLAB
    wcb_refs, wcc_refs, wcx_refs = refs[:n_slab], refs[n_slab:2 * n_slab], refs[2 * n_slab:3 * n_slab]
    cw_ref, o_ref, xe_ref, u_ref = refs[3 * n_slab:]
    i = pl.program_id(0)
    first = (i % n_seq) == 0
    last = (i % n_seq) == n_seq - 1
    xe_ref[0:HALO, :] = jnp.where(first, jnp.zeros_like(xp_ref[...]), xp_ref[...])
    xe_ref[HALO:HALO + tm, :] = xn_ref[...]
    xe_ref[HALO + tm:2 * HALO + tm, :] = jnp.where(last, jnp.zeros_like(xq_ref[...]), xq_ref[...])
    xe = xe_ref[...]
    for c in range(D_MODEL // tn):
        slab, ws = divmod(c * tn, W_SLAB)
        ws = slice(ws, ws + tn)
        cs = slice(c * tn, (c + 1) * tn)
        u_ref[...] = (jnp.dot(xe, wcc_refs[slab][:, ws], preferred_element_type=F32)
                      * jnp.dot(xe, wcx_refs[slab][:, ws], preferred_element_type=F32))
        cb = jnp.dot(xe_ref[HALO:HALO + tm, :], wcb_refs[slab][:, ws], preferred_element_type=F32)
        cw = cw_ref[:, cs]
        conv = (cw[0:1, :] * u_ref[HALO - 1:HALO - 1 + tm, :]
                + cw[1:2, :] * u_ref[HALO:HALO + tm, :]
                + cw[2:3, :] * u_ref[HALO + 1:HALO + 1 + tm, :])
        o_ref[:, cs] = (cb * conv).astype(BF16)


def _conv_call(xn, w_in, conv_w, seq):
    T = xn.shape[0]
    tm, tn = TM_PROJ, TN_PROJ
    n_seq = seq // tm
    halo_per_tile = tm // HALO
    n_halo_blocks = T // HALO
    n_slab = D_MODEL // W_SLAB

    def w_specs(off):
        return [pl.BlockSpec((D_MODEL, W_SLAB), lambda i, blk=off // W_SLAB + s: (0, blk),
                             pipeline_mode=pl.Buffered(1)) for s in range(n_slab)]

    return pl.pallas_call(
        functools.partial(_conv_kernel, n_seq=n_seq, tm=tm, tn=tn),
        grid=(T // tm,),
        in_specs=[
            pl.BlockSpec((tm, D_MODEL), lambda i: (i, 0)),
            pl.BlockSpec((HALO, D_MODEL), lambda i: (jnp.maximum(i * halo_per_tile - 1, 0), 0)),
            pl.BlockSpec((HALO, D_MODEL), lambda i: (jnp.minimum((i + 1) * halo_per_tile, n_halo_blocks - 1), 0)),
            *w_specs(OFF_CB), *w_specs(OFF_CC), *w_specs(OFF_CX),
            pl.BlockSpec((3, D_MODEL), lambda i: (0, 0)),
        ],
        out_specs=pl.BlockSpec((tm, D_MODEL), lambda i: (i, 0)),
        out_shape=jax.ShapeDtypeStruct((T, D_MODEL), BF16),
        scratch_shapes=[pltpu.VMEM((tm + 2 * HALO, D_MODEL), BF16),
                        pltpu.VMEM((tm + 2 * HALO, tn), F32)],
        name="conv_proj",
        compiler_params=_cparams(("arbitrary",)),
    )(xn, xn, xn, *([w_in] * (3 * n_slab)), conv_w)


def _attn_kernel(q_ref, k_ref, v_ref, o_ref, m_ref, l_ref, acc_ref, *, tq, tk, n_kv):
    q4 = jnp.concatenate([q_ref[:, g * HEAD_DIM:(g + 1) * HEAD_DIM] for g in range(GROUPS)], axis=0)
    m_ref[...] = jnp.full(m_ref.shape, -jnp.inf, F32)
    l_ref[...] = jnp.zeros(l_ref.shape, F32)
    acc_ref[...] = jnp.zeros(acc_ref.shape, F32)

    def body(c, carry):
        start = pl.multiple_of(c * tk, tk)
        kc = k_ref[pl.ds(start, tk), :]
        vc = v_ref[pl.ds(start, tk), :]
        s = lax.dot_general(q4, kc, (((1,), (1,)), ((), ())), preferred_element_type=F32)
        m_prev = m_ref[...]
        m_new = jnp.maximum(m_prev, jnp.max(s, axis=-1, keepdims=True))
        p = jnp.exp(s - m_new)
        alpha = jnp.exp(m_prev - m_new)
        l_ref[...] = alpha * l_ref[...] + jnp.sum(p, axis=-1, keepdims=True)
        acc_ref[...] = alpha * acc_ref[...] + jnp.dot(p.astype(BF16), vc, preferred_element_type=F32)
        m_ref[...] = m_new
        return carry

    lax.fori_loop(0, n_kv, body, 0)
    out = acc_ref[...] / l_ref[...]
    for g in range(GROUPS):
        o_ref[:, g * HEAD_DIM:(g + 1) * HEAD_DIM] = out[g * tq:(g + 1) * tq, :].astype(BF16)


def _attn_call(q, k, v, batch, seq):
    T = q.shape[0]
    tq = min(TQ_ATTN, seq)
    tk = min(TK_ATTN, seq)
    n_q = seq // tq
    gw = GROUPS * HEAD_DIM
    return pl.pallas_call(
        functools.partial(_attn_kernel, tq=tq, tk=tk, n_kv=seq // tk),
        grid=(batch, N_KV_HEADS, n_q),
        in_specs=[
            pl.BlockSpec((tq, gw), lambda b, h, i: (b * n_q + i, h)),
            pl.BlockSpec((seq, HEAD_DIM), lambda b, h, i: (b, h)),
            pl.BlockSpec((seq, HEAD_DIM), lambda b, h, i: (b, h)),
        ],
        out_specs=pl.BlockSpec((tq, gw), lambda b, h, i: (b * n_q + i, h)),
        out_shape=jax.ShapeDtypeStruct((T, ATTN_WIDTH), BF16),
        scratch_shapes=[pltpu.VMEM((GROUPS * tq, 1), F32),
                        pltpu.VMEM((GROUPS * tq, 1), F32),
                        pltpu.VMEM((GROUPS * tq, HEAD_DIM), F32)],
        name="gqa_attention",
        compiler_params=_cparams(("arbitrary", "arbitrary", "arbitrary")),
    )(q, k, v)


V_ROWS = HEAD_DIM + BF16_ROWS


def _attn_bounded_kernel(nm_ref, q_ref, k_ref, v_ref, o_ref, ka_ref, vt_ref, acc_ref, *, tq, tk, n_kv):
    @pl.when(pl.program_id(2) == 0)
    def _():
        lane = lax.broadcasted_iota(jnp.int32, k_ref.shape, 1)
        ka_ref[:, :HEAD_DIM] = k_ref[...]
        ka_ref[:, HEAD_DIM:] = jnp.where(lane == 0, 1.0, 0.0).astype(BF16)
        vt_ref[:HEAD_DIM, :] = v_ref[...].astype(F32).T.astype(BF16)
        vt_ref[HEAD_DIM:, :] = jnp.ones((BF16_ROWS, vt_ref.shape[1]), BF16)

    lane = lax.broadcasted_iota(jnp.int32, (tq, HEAD_DIM), 1)
    shift = jnp.where(lane == 0, nm_ref[0], 0.0).astype(BF16)
    q4 = jnp.concatenate(
        [jnp.concatenate([q_ref[:, g * HEAD_DIM:(g + 1) * HEAD_DIM], shift], axis=1) for g in range(GROUPS)],
        axis=0)
    acc_ref[...] = jnp.zeros(acc_ref.shape, F32)

    def body(c, carry):
        start = pl.multiple_of(c * tk, tk)
        s_t = lax.dot_general(ka_ref[pl.ds(start, tk), :], q4, (((1,), (1,)), ((), ())),
                              preferred_element_type=F32)
        p_t = jnp.exp(s_t).astype(BF16)
        acc_ref[...] += jnp.dot(vt_ref[:, pl.ds(start, tk)], p_t, preferred_element_type=F32)
        return carry

    lax.fori_loop(0, n_kv, body, 0, unroll=True)
    out = (acc_ref[:HEAD_DIM, :] / acc_ref[HEAD_DIM:HEAD_DIM + 1, :]).T
    for g in range(GROUPS):
        o_ref[:, g * HEAD_DIM:(g + 1) * HEAD_DIM] = out[g * tq:(g + 1) * tq, :].astype(BF16)


def _attn_bounded_call(neg_shift, q, k, v, batch, seq):
    T = q.shape[0]
    tq = min(TQ_ATTN_BOUNDED, seq)
    tk = min(TK_ATTN_BOUNDED, seq)
    n_q = seq // tq
    gw = GROUPS * HEAD_DIM
    return pl.pallas_call(
        functools.partial(_attn_bounded_kernel, tq=tq, tk=tk, n_kv=seq // tk),
        grid=(batch, N_KV_HEADS, n_q),
        in_specs=[
            pl.BlockSpec(memory_space=pltpu.SMEM),
            pl.BlockSpec((tq, gw), lambda b, h, i: (b * n_q + i, h)),
            pl.BlockSpec((seq, HEAD_DIM), lambda b, h, i: (b, h)),
            pl.BlockSpec((seq, HEAD_DIM), lambda b, h, i: (b, h)),
        ],
        out_specs=pl.BlockSpec((tq, gw), lambda b, h, i: (b * n_q + i, h)),
        out_shape=jax.ShapeDtypeStruct((T, ATTN_WIDTH), BF16),
        scratch_shapes=[pltpu.VMEM((seq, 2 * HEAD_DIM), BF16),
                        pltpu.VMEM((V_ROWS, seq), BF16),
                        pltpu.VMEM((V_ROWS, GROUPS * tq), F32)],
        name="gqa_attention_bounded",
        compiler_params=_cparams(("arbitrary", "arbitrary", "arbitrary")),
    )(neg_shift, q, k, v)


MAX_SCORE_SHIFT = 40.0


def _score_bound(q_norm, k_norm):
    slack = 1.02
    return slack * HEAD_DIM * ATTN_SCALE * jnp.max(jnp.abs(q_norm)) * jnp.max(jnp.abs(k_norm))


def _attention(q, k, v, q_norm, k_norm, batch, seq):
    m = _score_bound(q_norm, k_norm)
    return lax.cond(
        m <= MAX_SCORE_SHIFT,
        lambda: _attn_bounded_call((-m).reshape(1).astype(F32), q, k, v, batch, seq),
        lambda: _attn_call(q, k, v, batch, seq))


def _mix_out_kernel(x_ref, lng_ref, lnb_ref, a_ref, c_ref, g_ref, wa_ref, wc_ref, wo_ref, g1_ref, b1_ref,
                    wrh_ref, wrl_ref, br_ref, cnt_in_ref, x1_ref, x1p_ref, idx_ref, gate_ref, rank_ref, cnt_out_ref,
                    cnt_ref):
    att = jnp.dot(a_ref[...], wa_ref[...], preferred_element_type=F32)
    cnv = jnp.dot(c_ref[...], wc_ref[...], preferred_element_type=F32)
    merged = (g_ref[:, :D_MODEL].astype(F32) * att + g_ref[:, D_MODEL:].astype(F32) * cnv).astype(BF16)
    out = jnp.dot(merged, wo_ref[...], preferred_element_type=F32)
    xn = _layer_norm(x_ref[...], lng_ref[...], lnb_ref[...])
    x1 = _layer_norm(DEEPNORM_ALPHA * xn + out, g1_ref[...], b1_ref[...])
    x1_ref[...] = x1
    x1p_ref[...] = _pack_bf16_pairs(x1)
    x_hi = x1.astype(BF16)
    x_lo = (x1 - x_hi.astype(F32)).astype(BF16)
    logits = (jnp.dot(x_hi, wrh_ref[...], preferred_element_type=F32)
              + jnp.dot(x_hi, wrl_ref[...], preferred_element_type=F32)
              + jnp.dot(x_lo, wrh_ref[...], preferred_element_type=F32)) + br_ref[...]
    lane = lax.broadcasted_iota(jnp.int32, logits.shape, 1).astype(F32)
    vals = logits
    tops, ids = [], []
    for _ in range(TOP_K):
        mx = jnp.max(vals, axis=-1, keepdims=True)
        ix = jnp.min(jnp.where(vals == mx, lane, float(N_EXPERTS)), axis=-1, keepdims=True)
        tops.append(mx)
        ids.append(ix)
        vals = jnp.where(lane == ix, -jnp.inf, vals)
    exps = [jnp.exp(t - tops[0]) for t in tops]
    den = exps[0] + exps[1] + exps[2] + exps[3]
    out_lane = lax.broadcasted_iota(jnp.int32, idx_ref.shape, 1)
    idx_out = jnp.zeros(idx_ref.shape, F32)
    gate_out = jnp.zeros(gate_ref.shape, F32)
    for kk in range(TOP_K):
        idx_out = jnp.where(out_lane == kk, ids[kk], idx_out)
        gate_out = jnp.where(out_lane == kk, exps[kk] / den, gate_out)
    idx_ref[...] = idx_out.astype(jnp.int32)
    gate_ref[...] = gate_out

    @pl.when(pl.program_id(0) == 0)
    def _():
        cnt_ref[...] = cnt_in_ref[...]

    tm = logits.shape[0]
    picks = [(lane == ids[kk]).astype(F32) for kk in range(TOP_K)]
    per_tok = picks[0] + picks[1] + picks[2] + picks[3]
    earlier = (lax.broadcasted_iota(jnp.int32, (tm, tm), 1)
               < lax.broadcasted_iota(jnp.int32, (tm, tm), 0)).astype(BF16)
    before = cnt_ref[...] + jnp.dot(earlier, per_tok.astype(BF16), preferred_element_type=F32)
    rank_out = jnp.zeros(rank_ref.shape, F32)
    for kk in range(TOP_K):
        rank_out = jnp.where(out_lane == kk, jnp.sum(picks[kk] * before, axis=-1, keepdims=True), rank_out)
    rank_ref[...] = rank_out.astype(jnp.int32)
    cnt_ref[...] += jnp.sum(per_tok, axis=0, keepdims=True)
    cnt_out_ref[...] = cnt_ref[...]


def _mix_out_call(x, lng, lnb, attn, conv_in, gates, wa, wc, wo, g1, b1, wr_hi, wr_lo, br, counts):
    T = x.shape[0]
    tm = TM_OUT
    row = lambda n: pl.BlockSpec((tm, n), lambda i: (i, 0))
    vec = lambda n: pl.BlockSpec((1, n), lambda i: (0, 0))
    resident = lambda r, c: pl.BlockSpec((r, c), lambda i: (0, 0), pipeline_mode=pl.Buffered(1))
    return pl.pallas_call(
        _mix_out_kernel,
        grid=(T // tm,),
        in_specs=[row(D_MODEL), vec(D_MODEL), vec(D_MODEL),
                  row(ATTN_WIDTH), row(D_MODEL), row(2 * D_MODEL),
                  resident(ATTN_WIDTH, D_MODEL), resident(D_MODEL, D_MODEL), resident(D_MODEL, D_MODEL),
                  vec(D_MODEL), vec(D_MODEL),
                  pl.BlockSpec((D_MODEL, N_EXPERTS), lambda i: (0, 0)),
                  pl.BlockSpec((D_MODEL, N_EXPERTS), lambda i: (0, 0)),
                  vec(N_EXPERTS), vec(N_EXPERTS)],
        out_specs=[row(D_MODEL), row(D_MODEL // 2), row(LANES), row(LANES), row(LANES), vec(N_EXPERTS)],
        out_shape=[jax.ShapeDtypeStruct((T, D_MODEL), F32),
                   jax.ShapeDtypeStruct((T, D_MODEL // 2), jnp.uint32),
                   jax.ShapeDtypeStruct((T, LANES), jnp.int32),
                   jax.ShapeDtypeStruct((T, LANES), F32),
                   jax.ShapeDtypeStruct((T, LANES), jnp.int32),
                   jax.ShapeDtypeStruct((1, N_EXPERTS), F32)],
        scratch_shapes=[pltpu.VMEM((1, N_EXPERTS), F32)],
        name="mix_out_router",
        compiler_params=_cparams(("arbitrary",)),
    )(x, lng, lnb, attn, conv_in, gates, wa, wc, wo, g1, b1, wr_hi, wr_lo, br, counts)


def _dispatch_copy(stage, par, r, dst_hbm, slot_row, sem):
    return pltpu.make_async_copy(stage.at[par, pl.ds(r, 1), :], dst_hbm.at[pl.ds(slot_row, 1), :], sem.at[par])


def _dispatch_wait(stage, dst_hbm, sem, par, n):
    def body(r, c):
        _dispatch_copy(stage, par, 0, dst_hbm, 0, sem).wait()
        return c
    lax.fori_loop(0, n, body, 0, unroll=8)


def _dispatch_kernel(dest_ref, x_ref, xs_in_hbm, xs_hbm, stage, sem, *, tm, n_tiles):
    del xs_in_hbm
    i = pl.program_id(0)
    par = i % 2
    n = TOP_K * tm

    @pl.when(i >= 2)
    def _():
        _dispatch_wait(stage, xs_hbm, sem, par, n)

    stage[par] = x_ref[...]
    for r in range(tm):
        for kk in range(TOP_K):
            _dispatch_copy(stage, par, r, xs_hbm, dest_ref[0, 0, r * TOP_K + kk], sem).start()

    @pl.when(i == n_tiles - 1)
    def _():
        _dispatch_wait(stage, xs_hbm, sem, par, n)

        @pl.when(i >= 1)
        def _():
            _dispatch_wait(stage, xs_hbm, sem, 1 - par, n)


def _dispatch_call(dest, x1p, xs):
    T, width = x1p.shape
    tm = TM_DISPATCH
    n_tiles = T // tm
    d3 = dest.reshape(n_tiles, 1, TOP_K * tm)
    return pl.pallas_call(
        functools.partial(_dispatch_kernel, tm=tm, n_tiles=n_tiles),
        grid=(n_tiles,),
        in_specs=[
            pl.BlockSpec((1, 1, TOP_K * tm), lambda i: (i, 0, 0), memory_space=pltpu.SMEM),
            pl.BlockSpec((tm, width), lambda i: (i, 0)),
            pl.BlockSpec(memory_space=pl.ANY),
        ],
        out_specs=pl.BlockSpec(memory_space=pl.ANY),
        out_shape=jax.ShapeDtypeStruct(xs.shape, xs.dtype),
        scratch_shapes=[pltpu.VMEM((2, tm, width), x1p.dtype), pltpu.SemaphoreType.DMA((2,))],
        input_output_aliases={2: 0},
        name="moe_dispatch",
        compiler_params=_cparams(("arbitrary",)),
    )(d3, x1p, xs)


def _expert_kernel(te_ref, nu_ref, xs_ref, wgu_ref, bgu_ref, wd_ref, bd_ref, o_ref, xb_ref, *, fc):
    i = pl.program_id(0)
    n_used = nu_ref[0]
    half = D_MODEL // 2

    @pl.when(i < n_used)
    def _():
        lo, hi = _unpack_bf16_pairs(xs_ref[...])
        xb_ref[:, :half] = lo
        xb_ref[:, half:] = hi
        xb = xb_ref[...]
        acc = jnp.broadcast_to(bd_ref[...], (xb.shape[0], D_MODEL))
        for c in range(D_EXPERT // fc):
            cg = slice(c * fc, (c + 1) * fc)
            cu = slice(D_EXPERT + c * fc, D_EXPERT + (c + 1) * fc)
            hg = jnp.dot(xb, wgu_ref[:, cg], preferred_element_type=F32) + bgu_ref[:, cg]
            hu = jnp.dot(xb, wgu_ref[:, cu], preferred_element_type=F32) + bgu_ref[:, cu]
            g = jnp.minimum(hg, SWIGLU_LIMIT)
            u = jnp.clip(hu, -SWIGLU_LIMIT, SWIGLU_LIMIT)
            act = (u + 1.0) * (g * jax.nn.sigmoid(SWIGLU_ALPHA * g))
            acc = acc + jnp.dot(act.astype(BF16), wd_ref[cg, :], preferred_element_type=F32)
        o_ref[...] = _pack_bf16_pairs(acc)

    @pl.when(i >= n_used)
    def _():
        o_ref[...] = jnp.zeros(o_ref.shape, o_ref.dtype)


def _expert_call(tile_e, n_used, xs, wgu, bgu, wd, bd):
    n_slots = xs.shape[0]
    tm = TM_MOE
    n_tiles = n_slots // tm

    def live(i, nu):
        return jnp.minimum(i, nu[0] - 1)

    once = pl.Buffered(1)
    grid_spec = pltpu.PrefetchScalarGridSpec(
        num_scalar_prefetch=2,
        grid=(n_tiles,),
        in_specs=[
            pl.BlockSpec((tm, D_MODEL // 2), lambda i, te, nu: (live(i, nu), 0)),
            pl.BlockSpec((None, D_MODEL, 2 * D_EXPERT), lambda i, te, nu: (te[i], 0, 0), pipeline_mode=once),
            pl.BlockSpec((None, 1, 2 * D_EXPERT), lambda i, te, nu: (te[i], 0, 0)),
            pl.BlockSpec((None, D_EXPERT, D_MODEL), lambda i, te, nu: (te[i], 0, 0)),
            pl.BlockSpec((None, 1, D_MODEL), lambda i, te, nu: (te[i], 0, 0)),
        ],
        out_specs=pl.BlockSpec((tm, D_MODEL // 2), lambda i, te, nu: (i, 0)),
        scratch_shapes=[pltpu.VMEM((tm, D_MODEL), BF16)],
    )
    return pl.pallas_call(
        functools.partial(_expert_kernel, fc=FC_MOE),
        grid_spec=grid_spec,
        out_shape=jax.ShapeDtypeStruct((n_slots, D_MODEL // 2), jnp.uint32),
        name="expert_ffn",
        compiler_params=_cparams(("arbitrary",)),
    )(tile_e, n_used, xs, wgu, bgu, wd, bd)


def _row_copy(src_hbm, row, buf, slot, r, sem):
    return pltpu.make_async_copy(src_hbm.at[pl.ds(row, 1), :], buf.at[slot, pl.ds(r, 1), :], sem.at[slot])


def _start_rows(idx_ref, src_hbm, buf, slot, sem, n_rows):
    for r in range(n_rows):
        _row_copy(src_hbm, idx_ref[0, 0, r], buf, slot, r, sem).start()


def _wait_rows(src_hbm, buf, slot, sem, n_rows):
    def body(r, c):
        _row_copy(src_hbm, 0, buf, slot, r, sem).wait()
        return c
    lax.fori_loop(0, n_rows, body, 0, unroll=8)


def _combine_kernel(d0_ref, dn_ref, e_hbm, x1_ref, gate_ref, g2_ref, b2_ref, y_ref, buf, sem, *, tm, n_tiles):
    i = pl.program_id(0)
    slot = i % 2
    n_rows = TOP_K * tm

    @pl.when(i == 0)
    def _():
        _start_rows(d0_ref, e_hbm, buf, 0, sem, n_rows)

    @pl.when(i + 1 < n_tiles)
    def _():
        _start_rows(dn_ref, e_hbm, buf, 1 - slot, sem, n_rows)

    _wait_rows(e_hbm, buf, slot, sem, n_rows)
    gate = gate_ref[...]
    ffn_lo = jnp.zeros((tm, D_MODEL // 2), F32)
    ffn_hi = jnp.zeros((tm, D_MODEL // 2), F32)
    for kk in range(TOP_K):
        lo, hi = _unpack_pairs_f32(buf[slot, kk * tm:(kk + 1) * tm, :])
        ffn_lo = ffn_lo + gate[:, kk:kk + 1] * lo
        ffn_hi = ffn_hi + gate[:, kk:kk + 1] * hi
    ffn = jnp.concatenate([ffn_lo, ffn_hi], axis=1)
    y_ref[...] = _layer_norm(DEEPNORM_ALPHA * x1_ref[...] + ffn, g2_ref[...], b2_ref[...])


def _combine_call(dest, eout, x1, gate_pad, g2, b2):
    T = x1.shape[0]
    tm = TM_COMB
    n_tiles = T // tm
    d3 = dest.reshape(n_tiles, tm, TOP_K).transpose(0, 2, 1).reshape(n_tiles, 1, TOP_K * tm)
    smem_blk = lambda imap: pl.BlockSpec((1, 1, TOP_K * tm), imap, memory_space=pltpu.SMEM)
    vec = pl.BlockSpec((1, D_MODEL), lambda i: (0, 0))
    return pl.pallas_call(
        functools.partial(_combine_kernel, tm=tm, n_tiles=n_tiles),
        grid=(n_tiles,),
        in_specs=[
            smem_blk(lambda i: (0, 0, 0)),
            smem_blk(lambda i: (jnp.minimum(i + 1, n_tiles - 1), 0, 0)),
            pl.BlockSpec(memory_space=pl.ANY),
            pl.BlockSpec((tm, D_MODEL), lambda i: (i, 0)),
            pl.BlockSpec((tm, LANES), lambda i: (i, 0)),
            vec, vec,
        ],
        out_specs=pl.BlockSpec((tm, D_MODEL), lambda i: (i, 0)),
        out_shape=jax.ShapeDtypeStruct((T, D_MODEL), F32),
        scratch_shapes=[pltpu.VMEM((2, TOP_K * tm, D_MODEL // 2), jnp.uint32),
                        pltpu.SemaphoreType.DMA((2,))],
        name="combine_ln",
        compiler_params=_cparams(("arbitrary",)),
    )(d3, d3, eout, x1, gate_pad, g2, b2)


def _rope_tables(seq):
    pos = jnp.arange(seq, dtype=jnp.int32)
    row = (pos // GRID_W).astype(F32)
    col = (pos % GRID_W).astype(F32)
    inv_freq = ROPE_THETA ** (-jnp.arange(N_FREQ, dtype=F32) / N_FREQ)
    ang_r = row[:, None] * inv_freq[None, :]
    ang_c = col[:, None] * inv_freq[None, :]
    zero = jnp.zeros_like(ang_r)
    cos = jnp.concatenate([jnp.cos(ang_r)] * 2 + [jnp.cos(ang_c)] * 2, axis=-1)
    sa = jnp.concatenate([-jnp.sin(ang_r), zero, -jnp.sin(ang_c), zero], axis=-1)
    sb = jnp.concatenate([zero, jnp.sin(ang_r), zero, jnp.sin(ang_c)], axis=-1)
    return cos, sa, sb


def _dispatch_plan(idx, rank, counts, tm):
    T = idx.shape[0]
    n_tiles = T * TOP_K // tm + N_EXPERTS
    padded = (counts + tm - 1) // tm * tm
    pend = jnp.cumsum(padded)
    pstart = pend - padded
    onehot = idx[:, :, None] == jnp.arange(N_EXPERTS, dtype=jnp.int32)[None, None, :]
    dest = jnp.sum(jnp.where(onehot, pstart[None, None, :], 0), axis=-1) + rank
    n_used = (pend[-1] // tm).astype(jnp.int32)
    tile_start = jnp.minimum(jnp.arange(n_tiles, dtype=jnp.int32), n_used - 1) * tm
    tile_e = jnp.minimum(jnp.sum((tile_start[:, None] >= pend[None, :]).astype(jnp.int32), axis=1), N_EXPERTS - 1)
    return tile_e, n_used.reshape(1), dest, n_tiles * tm


def _mixers(x3, p, counts):
    batch, seq, _ = x3.shape
    x = x3.reshape(batch * seq, D_MODEL)
    cos, sa, sb = _rope_tables(seq)
    q, k, v, gates, xn = _qkvg_call(x, p["ln_in_g"], p["ln_in_b"], p["w_in"], cos, sa, sb,
                                    p["q_norm"], p["k_norm"], seq)
    conv_in = _conv_call(xn, p["w_in"], p["conv_w"], seq)
    attn = _attention(q, k, v, p["q_norm"], p["k_norm"], batch, seq)
    return _mix_out_call(x, p["ln_in_g"], p["ln_in_b"], attn, conv_in, gates,
                         p["w_attn_out"], p["w_conv_out"], p["w_o"],
                         p["ln1_g"], p["ln1_b"], p["w_router_hi"], p["w_router_lo"], p["b_router"], counts)


def _trunks(xs3, p):
    counts = jnp.zeros((1, N_EXPERTS), F32)
    mixed = []
    for x3 in xs3:
        mixed.append(_mixers(x3, p, counts))
        counts = mixed[-1][5]
    idx_all = jnp.concatenate([m[2][:, :TOP_K] for m in mixed], axis=0)
    rank_all = jnp.concatenate([m[4][:, :TOP_K] for m in mixed], axis=0)
    tile_e, n_used, dest_all, n_slots = _dispatch_plan(idx_all, rank_all, counts[0].astype(jnp.int32), TM_MOE)
    slots = jnp.zeros((n_slots, D_MODEL // 2), jnp.uint32)
    row0 = 0
    dests = []
    for m in mixed:
        n_tok = m[0].shape[0]
        dests.append(dest_all[row0:row0 + n_tok])
        slots = _dispatch_call(dests[-1], m[1], slots)
        row0 += n_tok
    eout = _expert_call(tile_e, n_used, slots, p["w_gate_up"], p["b_gate_up"], p["w_down"], p["b_down"])
    outs = []
    for x3, m, dest in zip(xs3, mixed, dests):
        y = _combine_call(dest, eout, m[0], m[3], p["ln2_g"], p["ln2_b"])
        outs.append(y.reshape(x3.shape))
    return tuple(outs)


def _prepare(ln_in_g, ln_in_b, w_in, q_norm, k_norm, conv_w, w_attn_out, w_conv_out, w_o,
             ln1_g, ln1_b, w_router, b_router, w_gate_up, b_gate_up, w_down, b_down, ln2_g, ln2_b):
    assert w_in.shape[0] == DEPTH
    vec = lambda a: a.reshape(1, -1)
    return {
        "ln_in_g": vec(ln_in_g), "ln_in_b": vec(ln_in_b),
        "w_in": w_in[0].astype(BF16),
        "q_norm": vec(q_norm[0]), "k_norm": vec(k_norm[0]),
        "conv_w": conv_w[0],
        "w_attn_out": w_attn_out[0].astype(BF16),
        "w_conv_out": w_conv_out[0].astype(BF16),
        "w_o": w_o[0].astype(BF16),
        "ln1_g": vec(ln1_g[0]), "ln1_b": vec(ln1_b[0]),
        "w_router_hi": w_router[0].astype(BF16),
        "w_router_lo": (w_router[0] - w_router[0].astype(BF16).astype(F32)).astype(BF16),
        "b_router": vec(b_router[0]),
        "w_gate_up": w_gate_up[0].astype(BF16),
        "b_gate_up": b_gate_up[0].reshape(N_EXPERTS, 1, 2 * D_EXPERT),
        "w_down": w_down[0].astype(BF16),
        "b_down": b_down[0].reshape(N_EXPERTS, 1, D_MODEL),
        "ln2_g": vec(ln2_g[0]), "ln2_b": vec(ln2_b[0]),
    }


def kernel(x_prompt, x_sample, ln_in_g, ln_in_b, w_in, q_norm, k_norm, conv_w, w_attn_out, w_conv_out, w_o,
           ln1_g, ln1_b, w_router, b_router, w_gate_up, b_gate_up, w_down, b_down, ln2_g, ln2_b):
    p = _prepare(ln_in_g, ln_in_b, w_in, q_norm, k_norm, conv_w, w_attn_out, w_conv_out, w_o,
                 ln1_g, ln1_b, w_router, b_router, w_gate_up, b_gate_up, w_down, b_down, ln2_g, ln2_b)
    return _trunks((x_prompt, x_sample), p)
```

```python
import functools

import jax
import jax.numpy as jnp
from jax import lax
from jax.experimental import pallas as pl
from jax.experimental.pallas import tpu as pltpu

F32 = jnp.float32
BF16 = jnp.bfloat16

D_MODEL = 2048
GRID_W = 64
N_HEADS = 16
N_KV_HEADS = 4
GROUPS = N_HEADS // N_KV_HEADS
HEAD_DIM = 128
ROT_HALF = HEAD_DIM // 2
N_FREQ = ROT_HALF // 2
ROPE_THETA = 10000.0
ATTN_WIDTH = N_HEADS * HEAD_DIM
KV_WIDTH = N_KV_HEADS * HEAD_DIM
N_EXPERTS = 32
TOP_K = 4
D_EXPERT = D_MODEL
SWIGLU_LIMIT = 7.0
SWIGLU_ALPHA = 1.702
RMS_EPS = 1e-6
LN_EPS = 1e-5
DEPTH = 1
DEEPNORM_ALPHA = (2 * DEPTH) ** 0.25
ATTN_SCALE = HEAD_DIM ** -0.5

OFF_K = ATTN_WIDTH
OFF_V = OFF_K + KV_WIDTH
OFF_CB = OFF_V + KV_WIDTH
OFF_CC = OFF_CB + D_MODEL
OFF_CX = OFF_CC + D_MODEL
OFF_GA = OFF_CX + D_MODEL
OFF_GC = OFF_GA + D_MODEL

LANES = 128
BF16_ROWS = 16
F32_ROWS = 8
VMEM_LIMIT = 56 * 1024 * 1024

TM_QKV = 1024
TM_PROJ = 512
TN_PROJ = 512
TQ_ATTN = 256
TQ_ATTN_BOUNDED = 512
TK_ATTN = 512
TK_ATTN_BOUNDED = 1024
TM_OUT = 256
TM_MOE = 512
FC_MOE = 512
TM_COMB = 128
TM_DISPATCH = 128
HALO = BF16_ROWS


def _cparams(sem):
    return pltpu.CompilerParams(dimension_semantics=sem, vmem_limit_bytes=VMEM_LIMIT)


def _pack_bf16_pairs(x):
    half = x.shape[1] // 2
    lo = lax.bitcast_convert_type(x[:, :half].astype(BF16).astype(F32), jnp.uint32)
    hi = lax.bitcast_convert_type(x[:, half:].astype(BF16).astype(F32), jnp.uint32)
    return (lo >> 16) | (hi & jnp.uint32(0xFFFF0000))


def _unpack_pairs_f32(w):
    lo = lax.bitcast_convert_type(w << 16, F32)
    hi = lax.bitcast_convert_type(w & jnp.uint32(0xFFFF0000), F32)
    return lo, hi


def _unpack_bf16_pairs(w):
    lo, hi = _unpack_pairs_f32(w)
    return lo.astype(BF16), hi.astype(BF16)


def _layer_norm(x, g, b):
    mu = jnp.mean(x, axis=-1, keepdims=True)
    xc = x - mu
    var = jnp.mean(xc * xc, axis=-1, keepdims=True)
    return xc * lax.rsqrt(var + LN_EPS) * g + b


N_QT = ATTN_WIDTH // TN_PROJ
J_K = N_QT
J_V = N_QT + 1
J_G = N_QT + 2
N_GT = 2 * D_MODEL // TN_PROJ
N_J = J_G + N_GT


def _qkvg_kernel(x_ref, lng_ref, lnb_ref, w_ref, cos_ref, sa_ref, sb_ref, qn_ref, kn_ref,
                 q_ref, k_ref, v_ref, g_ref, xn_ref, acc0_ref, acc1_ref):
    j = pl.program_id(1)
    accs = (acc0_ref, acc1_ref)

    def norm_rope(h, gain, scale):
        ms = jnp.mean(h * h, axis=-1, keepdims=True)
        hn = h * lax.rsqrt(ms + RMS_EPS) * gain
        r = (hn * cos_ref[...] + pltpu.roll(hn, HEAD_DIM - N_FREQ, 1) * sa_ref[...]
             + pltpu.roll(hn, N_FREQ, 1) * sb_ref[...])
        return r * scale

    def finish(jj, acc):
        if jj < J_K:
            for a in range(TN_PROJ // HEAD_DIM):
                sl = slice(a * HEAD_DIM, (a + 1) * HEAD_DIM)
                q_ref[:, sl] = norm_rope(acc[:, sl], qn_ref[...], ATTN_SCALE).astype(BF16)
        elif jj == J_K:
            for a in range(N_KV_HEADS):
                sl = slice(a * HEAD_DIM, (a + 1) * HEAD_DIM)
                k_ref[:, sl] = norm_rope(acc[:, sl], kn_ref[...], 1.0).astype(BF16)
        elif jj == J_V:
            for a in range(N_KV_HEADS):
                sl = slice(a * HEAD_DIM, (a + 1) * HEAD_DIM)
                v_ref[:, sl] = acc[:, sl].astype(BF16)
        else:
            for a in range(TN_PROJ // HEAD_DIM):
                sl = slice(a * HEAD_DIM, (a + 1) * HEAD_DIM)
                g_ref[:, sl] = jax.nn.sigmoid(acc[:, sl]).astype(BF16)

    for step in range(N_J + 1):
        @pl.when(j == step)
        def _(step=step):
            if step == 0:
                xn_ref[...] = _layer_norm(x_ref[...], lng_ref[...], lnb_ref[...]).astype(BF16)
            if step < N_J:
                accs[step % 2][...] = jnp.dot(xn_ref[...], w_ref[...], preferred_element_type=F32)
            if step >= 1:
                finish(step - 1, accs[(step - 1) % 2])


def _qkvg_call(x, lng, lnb, w_in, cos, sa, sb, qn, kn, seq):
    T = x.shape[0]
    tm, tn = TM_QKV, TN_PROJ
    n_seq = seq // tm
    gate_blk0 = OFF_GA // tn

    def w_map(i, j):
        jc = jnp.minimum(j, N_J - 1)
        return (0, jnp.where(jc < J_G, jc, jc - J_G + gate_blk0))

    rope_spec = pl.BlockSpec((tm, HEAD_DIM), lambda i, j: (i % n_seq, 0))
    vec_spec = lambda n: pl.BlockSpec((1, n), lambda i, j: (0, 0))
    return pl.pallas_call(
        _qkvg_kernel,
        grid=(T // tm, N_J + 1),
        in_specs=[
            pl.BlockSpec((tm, D_MODEL), lambda i, j: (i, 0)),
            vec_spec(D_MODEL), vec_spec(D_MODEL),
            pl.BlockSpec((D_MODEL, tn), w_map),
            rope_spec, rope_spec, rope_spec,
            vec_spec(HEAD_DIM), vec_spec(HEAD_DIM),
        ],
        out_specs=[
            pl.BlockSpec((tm, tn), lambda i, j: (i, jnp.clip(j - 1, 0, N_QT - 1))),
            pl.BlockSpec((tm, KV_WIDTH), lambda i, j: (i, 0)),
            pl.BlockSpec((tm, KV_WIDTH), lambda i, j: (i, 0)),
            pl.BlockSpec((tm, tn), lambda i, j: (i, jnp.clip(j - 1 - J_G, 0, N_GT - 1))),
            pl.BlockSpec((tm, D_MODEL), lambda i, j: (i, 0)),
        ],
        out_shape=[
            jax.ShapeDtypeStruct((T, ATTN_WIDTH), BF16),
            jax.ShapeDtypeStruct((T, KV_WIDTH), BF16),
            jax.ShapeDtypeStruct((T, KV_WIDTH), BF16),
            jax.ShapeDtypeStruct((T, 2 * D_MODEL), BF16),
            jax.ShapeDtypeStruct((T, D_MODEL), BF16),
        ],
        scratch_shapes=[pltpu.VMEM((tm, tn), F32), pltpu.VMEM((tm, tn), F32)],
        name="qkv_gate_proj",
        compiler_params=_cparams(("arbitrary", "arbitrary")),
    )(x, lng, lnb, w_in, cos, sa, sb, qn, kn)


W_SLAB = 1024


def _conv_kernel(xn_ref, xp_ref, xq_ref, *refs, n_seq, tm, tn):
    n_slab = D_MODEL // W_SLAB
    wcb_refs, wcc_refs, wcx_refs = refs[:n_slab], refs[n_slab:2 * n_slab], refs[2 * n_slab:3 * n_slab]
    cw_ref, o_ref, xe_ref, u_ref = refs[3 * n_slab:]
    i = pl.program_id(0)
    first = (i % n_seq) == 0
    last = (i % n_seq) == n_seq - 1
    xe_ref[0:HALO, :] = jnp.where(first, jnp.zeros_like(xp_ref[...]), xp_ref[...])
    xe_ref[HALO:HALO + tm, :] = xn_ref[...]
    xe_ref[HALO + tm:2 * HALO + tm, :] = jnp.where(last, jnp.zeros_like(xq_ref[...]), xq_ref[...])
    xe = xe_ref[...]
    for c in range(D_MODEL // tn):
        slab, ws = divmod(c * tn, W_SLAB)
        ws = slice(ws, ws + tn)
        cs = slice(c * tn, (c + 1) * tn)
        u_ref[...] = (jnp.dot(xe, wcc_refs[slab][:, ws], preferred_element_type=F32)
                      * jnp.dot(xe, wcx_refs[slab][:, ws], preferred_element_type=F32))
        cb = jnp.dot(xe_ref[HALO:HALO + tm, :], wcb_refs[slab][:, ws], preferred_element_type=F32)
        cw = cw_ref[:, cs]
        conv = (cw[0:1, :] * u_ref[HALO - 1:HALO - 1 + tm, :]
                + cw[1:2, :] * u_ref[HALO:HALO + tm, :]
                + cw[2:3, :] * u_ref[HALO + 1:HALO + 1 + tm, :])
        o_ref[:, cs] = (cb * conv).astype(BF16)


def _conv_call(xn, w_in, conv_w, seq):
    T = xn.shape[0]
    tm, tn = TM_PROJ, TN_PROJ
    n_seq = seq // tm
    halo_per_tile = tm // HALO
    n_halo_blocks = T // HALO
    n_slab = D_MODEL // W_SLAB

    def w_specs(off):
        return [pl.BlockSpec((D_MODEL, W_SLAB), lambda i, blk=off // W_SLAB + s: (0, blk),
                             pipeline_mode=pl.Buffered(1)) for s in range(n_slab)]

    return pl.pallas_call(
        functools.partial(_conv_kernel, n_seq=n_seq, tm=tm, tn=tn),
        grid=(T // tm,),
        in_specs=[
            pl.BlockSpec((tm, D_MODEL), lambda i: (i, 0)),
            pl.BlockSpec((HALO, D_MODEL), lambda i: (jnp.maximum(i * halo_per_tile - 1, 0), 0)),
            pl.BlockSpec((HALO, D_MODEL), lambda i: (jnp.minimum((i + 1) * halo_per_tile, n_halo_blocks - 1), 0)),
            *w_specs(OFF_CB), *w_specs(OFF_CC), *w_specs(OFF_CX),
            pl.BlockSpec((3, D_MODEL), lambda i: (0, 0)),
        ],
        out_specs=pl.BlockSpec((tm, D_MODEL), lambda i: (i, 0)),
        out_shape=jax.ShapeDtypeStruct((T, D_MODEL), BF16),
        scratch_shapes=[pltpu.VMEM((tm + 2 * HALO, D_MODEL), BF16),
                        pltpu.VMEM((tm + 2 * HALO, tn), F32)],
        name="conv_proj",
        compiler_params=_cparams(("arbitrary",)),
    )(xn, xn, xn, *([w_in] * (3 * n_slab)), conv_w)


def _attn_kernel(q_ref, k_ref, v_ref, o_ref, m_ref, l_ref, acc_ref, *, tq, tk, n_kv):
    q4 = jnp.concatenate([q_ref[:, g * HEAD_DIM:(g + 1) * HEAD_DIM] for g in range(GROUPS)], axis=0)
    m_ref[...] = jnp.full(m_ref.shape, -jnp.inf, F32)
    l_ref[...] = jnp.zeros(l_ref.shape, F32)
    acc_ref[...] = jnp.zeros(acc_ref.shape, F32)

    def body(c, carry):
        start = pl.multiple_of(c * tk, tk)
        kc = k_ref[pl.ds(start, tk), :]
        vc = v_ref[pl.ds(start, tk), :]
        s = lax.dot_general(q4, kc, (((1,), (1,)), ((), ())), preferred_element_type=F32)
        m_prev = m_ref[...]
        m_new = jnp.maximum(m_prev, jnp.max(s, axis=-1, keepdims=True))
        p = jnp.exp(s - m_new)
        alpha = jnp.exp(m_prev - m_new)
        l_ref[...] = alpha * l_ref[...] + jnp.sum(p, axis=-1, keepdims=True)
        acc_ref[...] = alpha * acc_ref[...] + jnp.dot(p.astype(BF16), vc, preferred_element_type=F32)
        m_ref[...] = m_new
        return carry

    lax.fori_loop(0, n_kv, body, 0)
    out = acc_ref[...] / l_ref[...]
    for g in range(GROUPS):
        o_ref[:, g * HEAD_DIM:(g + 1) * HEAD_DIM] = out[g * tq:(g + 1) * tq, :].astype(BF16)


def _attn_call(q, k, v, batch, seq):
    T = q.shape[0]
    tq = min(TQ_ATTN, seq)
    tk = min(TK_ATTN, seq)
    n_q = seq // tq
    gw = GROUPS * HEAD_DIM
    return pl.pallas_call(
        functools.partial(_attn_kernel, tq=tq, tk=tk, n_kv=seq // tk),
        grid=(batch, N_KV_HEADS, n_q),
        in_specs=[
            pl.BlockSpec((tq, gw), lambda b, h, i: (b * n_q + i, h)),
            pl.BlockSpec((seq, HEAD_DIM), lambda b, h, i: (b, h)),
            pl.BlockSpec((seq, HEAD_DIM), lambda b, h, i: (b, h)),
        ],
        out_specs=pl.BlockSpec((tq, gw), lambda b, h, i: (b * n_q + i, h)),
        out_shape=jax.ShapeDtypeStruct((T, ATTN_WIDTH), BF16),
        scratch_shapes=[pltpu.VMEM((GROUPS * tq, 1), F32),
                        pltpu.VMEM((GROUPS * tq, 1), F32),
                        pltpu.VMEM((GROUPS * tq, HEAD_DIM), F32)],
        name="gqa_attention",
        compiler_params=_cparams(("arbitrary", "arbitrary", "arbitrary")),
    )(q, k, v)


V_ROWS = HEAD_DIM + BF16_ROWS


def _attn_bounded_kernel(nm_ref, q_ref, k_ref, v_ref, o_ref, ka_ref, vt_ref, acc_ref, *, tq, tk, n_kv):
    @pl.when(pl.program_id(2) == 0)
    def _():
        lane = lax.broadcasted_iota(jnp.int32, k_ref.shape, 1)
        ka_ref[:, :HEAD_DIM] = k_ref[...]
        ka_ref[:, HEAD_DIM:] = jnp.where(lane == 0, 1.0, 0.0).astype(BF16)
        vt_ref[:HEAD_DIM, :] = v_ref[...].astype(F32).T.astype(BF16)
        vt_ref[HEAD_DIM:, :] = jnp.ones((BF16_ROWS, vt_ref.shape[1]), BF16)

    lane = lax.broadcasted_iota(jnp.int32, (tq, HEAD_DIM), 1)
    shift = jnp.where(lane == 0, nm_ref[0], 0.0).astype(BF16)
    q4 = jnp.concatenate(
        [jnp.concatenate([q_ref[:, g * HEAD_DIM:(g + 1) * HEAD_DIM], shift], axis=1) for g in range(GROUPS)],
        axis=0)
    acc_ref[...] = jnp.zeros(acc_ref.shape, F32)

    def body(c, carry):
        start = pl.multiple_of(c * tk, tk)
        s_t = lax.dot_general(ka_ref[pl.ds(start, tk), :], q4, (((1,), (1,)), ((), ())),
                              preferred_element_type=F32)
        p_t = jnp.exp(s_t).astype(BF16)
        acc_ref[...] += jnp.dot(vt_ref[:, pl.ds(start, tk)], p_t, preferred_element_type=F32)
        return carry

    lax.fori_loop(0, n_kv, body, 0, unroll=True)
    out = (acc_ref[:HEAD_DIM, :] / acc_ref[HEAD_DIM:HEAD_DIM + 1, :]).T
    for g in range(GROUPS):
        o_ref[:, g * HEAD_DIM:(g + 1) * HEAD_DIM] = out[g * tq:(g + 1) * tq, :].astype(BF16)


def _attn_bounded_call(neg_shift, q, k, v, batch, seq):
    T = q.shape[0]
    tq = min(TQ_ATTN_BOUNDED, seq)
    tk = min(TK_ATTN_BOUNDED, seq)
    n_q = seq // tq
    gw = GROUPS * HEAD_DIM
    return pl.pallas_call(
        functools.partial(_attn_bounded_kernel, tq=tq, tk=tk, n_kv=seq // tk),
        grid=(batch, N_KV_HEADS, n_q),
        in_specs=[
            pl.BlockSpec(memory_space=pltpu.SMEM),
            pl.BlockSpec((tq, gw), lambda b, h, i: (b * n_q + i, h)),
            pl.BlockSpec((seq, HEAD_DIM), lambda b, h, i: (b, h)),
            pl.BlockSpec((seq, HEAD_DIM), lambda b, h, i: (b, h)),
        ],
        out_specs=pl.BlockSpec((tq, gw), lambda b, h, i: (b * n_q + i, h)),
        out_shape=jax.ShapeDtypeStruct((T, ATTN_WIDTH), BF16),
        scratch_shapes=[pltpu.VMEM((seq, 2 * HEAD_DIM), BF16),
                        pltpu.VMEM((V_ROWS, seq), BF16),
                        pltpu.VMEM((V_ROWS, GROUPS * tq), F32)],
        name="gqa_attention_bounded",
        compiler_params=_cparams(("arbitrary", "arbitrary", "arbitrary")),
    )(neg_shift, q, k, v)


MAX_SCORE_SHIFT = 40.0


def _score_bound(q_norm, k_norm):
    slack = 1.02
    return slack * HEAD_DIM * ATTN_SCALE * jnp.max(jnp.abs(q_norm)) * jnp.max(jnp.abs(k_norm))


def _attention(q, k, v, q_norm, k_norm, batch, seq):
    m = _score_bound(q_norm, k_norm)
    return lax.cond(
        m <= MAX_SCORE_SHIFT,
        lambda: _attn_bounded_call((-m).reshape(1).astype(F32), q, k, v, batch, seq),
        lambda: _attn_call(q, k, v, batch, seq))


def _mix_out_kernel(x_ref, lng_ref, lnb_ref, a_ref, c_ref, g_ref, wa_ref, wc_ref, wo_ref, g1_ref, b1_ref,
                    wrh_ref, wrl_ref, br_ref, cnt_in_ref, x1_ref, x1p_ref, idx_ref, gate_ref, rank_ref, cnt_out_ref,
                    cnt_ref):
    att = jnp.dot(a_ref[...], wa_ref[...], preferred_element_type=F32)
    cnv = jnp.dot(c_ref[...], wc_ref[...], preferred_element_type=F32)
    merged = (g_ref[:, :D_MODEL].astype(F32) * att + g_ref[:, D_MODEL:].astype(F32) * cnv).astype(BF16)
    out = jnp.dot(merged, wo_ref[...], preferred_element_type=F32)
    xn = _layer_norm(x_ref[...], lng_ref[...], lnb_ref[...])
    x1 = _layer_norm(DEEPNORM_ALPHA * xn + out, g1_ref[...], b1_ref[...])
    x1_ref[...] = x1
    x1p_ref[...] = _pack_bf16_pairs(x1)
    x_hi = x1.astype(BF16)
    x_lo = (x1 - x_hi.astype(F32)).astype(BF16)
    logits = (jnp.dot(x_hi, wrh_ref[...], preferred_element_type=F32)
              + jnp.dot(x_hi, wrl_ref[...], preferred_element_type=F32)
              + jnp.dot(x_lo, wrh_ref[...], preferred_element_type=F32)) + br_ref[...]
    lane = lax.broadcasted_iota(jnp.int32, logits.shape, 1).astype(F32)
    vals = logits
    tops, ids = [], []
    for _ in range(TOP_K):
        mx = jnp.max(vals, axis=-1, keepdims=True)
        ix = jnp.min(jnp.where(vals == mx, lane, float(N_EXPERTS)), axis=-1, keepdims=True)
        tops.append(mx)
        ids.append(ix)
        vals = jnp.where(lane == ix, -jnp.inf, vals)
    exps = [jnp.exp(t - tops[0]) for t in tops]
    den = exps[0] + exps[1] + exps[2] + exps[3]
    out_lane = lax.broadcasted_iota(jnp.int32, idx_ref.shape, 1)
    idx_out = jnp.zeros(idx_ref.shape, F32)
    gate_out = jnp.zeros(gate_ref.shape, F32)
    for kk in range(TOP_K):
        idx_out = jnp.where(out_lane == kk, ids[kk], idx_out)
        gate_out = jnp.where(out_lane == kk, exps[kk] / den, gate_out)
    idx_ref[...] = idx_out.astype(jnp.int32)
    gate_ref[...] = gate_out

    @pl.when(pl.program_id(0) == 0)
    def _():
        cnt_ref[...] = cnt_in_ref[...]

    tm = logits.shape[0]
    picks = [(lane == ids[kk]).astype(F32) for kk in range(TOP_K)]
    per_tok = picks[0] + picks[1] + picks[2] + picks[3]
    earlier = (lax.broadcasted_iota(jnp.int32, (tm, tm), 1)
               < lax.broadcasted_iota(jnp.int32, (tm, tm), 0)).astype(BF16)
    before = cnt_ref[...] + jnp.dot(earlier, per_tok.astype(BF16), preferred_element_type=F32)
    rank_out = jnp.zeros(rank_ref.shape, F32)
    for kk in range(TOP_K):
        rank_out = jnp.where(out_lane == kk, jnp.sum(picks[kk] * before, axis=-1, keepdims=True), rank_out)
    rank_ref[...] = rank_out.astype(jnp.int32)
    cnt_ref[...] += jnp.sum(per_tok, axis=0, keepdims=True)
    cnt_out_ref[...] = cnt_ref[...]


def _mix_out_call(x, lng, lnb, attn, conv_in, gates, wa, wc, wo, g1, b1, wr_hi, wr_lo, br, counts):
    T = x.shape[0]
    tm = TM_OUT
    row = lambda n: pl.BlockSpec((tm, n), lambda i: (i, 0))
    vec = lambda n: pl.BlockSpec((1, n), lambda i: (0, 0))
    resident = lambda r, c: pl.BlockSpec((r, c), lambda i: (0, 0), pipeline_mode=pl.Buffered(1))
    return pl.pallas_call(
        _mix_out_kernel,
        grid=(T // tm,),
        in_specs=[row(D_MODEL), vec(D_MODEL), vec(D_MODEL),
                  row(ATTN_WIDTH), row(D_MODEL), row(2 * D_MODEL),
                  resident(ATTN_WIDTH, D_MODEL), resident(D_MODEL, D_MODEL), resident(D_MODEL, D_MODEL),
                  vec(D_MODEL), vec(D_MODEL),
                  pl.BlockSpec((D_MODEL, N_EXPERTS), lambda i: (0, 0)),
                  pl.BlockSpec((D_MODEL, N_EXPERTS), lambda i: (0, 0)),
                  vec(N_EXPERTS), vec(N_EXPERTS)],
        out_specs=[row(D_MODEL), row(D_MODEL // 2), row(LANES), row(LANES), row(LANES), vec(N_EXPERTS)],
        out_shape=[jax.ShapeDtypeStruct((T, D_MODEL), F32),
                   jax.ShapeDtypeStruct((T, D_MODEL // 2), jnp.uint32),
                   jax.ShapeDtypeStruct((T, LANES), jnp.int32),
                   jax.ShapeDtypeStruct((T, LANES), F32),
                   jax.ShapeDtypeStruct((T, LANES), jnp.int32),
                   jax.ShapeDtypeStruct((1, N_EXPERTS), F32)],
        scratch_shapes=[pltpu.VMEM((1, N_EXPERTS), F32)],
        name="mix_out_router",
        compiler_params=_cparams(("arbitrary",)),
    )(x, lng, lnb, attn, conv_in, gates, wa, wc, wo, g1, b1, wr_hi, wr_lo, br, counts)


def _dispatch_copy(stage, par, r, dst_hbm, slot_row, sem):
    return pltpu.make_async_copy(stage.at[par, pl.ds(r, 1), :], dst_hbm.at[pl.ds(slot_row, 1), :], sem.at[par])


def _dispatch_wait(stage, dst_hbm, sem, par, n):
    def body(r, c):
        _dispatch_copy(stage, par, 0, dst_hbm, 0, sem).wait()
        return c
    lax.fori_loop(0, n, body, 0, unroll=8)


def _dispatch_kernel(dest_ref, x_ref, xs_in_hbm, xs_hbm, stage, sem, *, tm, n_tiles):
    del xs_in_hbm
    i = pl.program_id(0)
    par = i % 2
    n = TOP_K * tm

    @pl.when(i >= 2)
    def _():
        _dispatch_wait(stage, xs_hbm, sem, par, n)

    stage[par] = x_ref[...]
    for r in range(tm):
        for kk in range(TOP_K):
            _dispatch_copy(stage, par, r, xs_hbm, dest_ref[0, 0, r * TOP_K + kk], sem).start(priority=kk % 2)

    @pl.when(i == n_tiles - 1)
    def _():
        _dispatch_wait(stage, xs_hbm, sem, par, n)

        @pl.when(i >= 1)
        def _():
            _dispatch_wait(stage, xs_hbm, sem, 1 - par, n)


def _dispatch_call(dest, x1p, xs):
    T, width = x1p.shape
    tm = TM_DISPATCH
    n_tiles = T // tm
    d3 = dest.reshape(n_tiles, 1, TOP_K * tm)
    return pl.pallas_call(
        functools.partial(_dispatch_kernel, tm=tm, n_tiles=n_tiles),
        grid=(n_tiles,),
        in_specs=[
            pl.BlockSpec((1, 1, TOP_K * tm), lambda i: (i, 0, 0), memory_space=pltpu.SMEM),
            pl.BlockSpec((tm, width), lambda i: (i, 0)),
            pl.BlockSpec(memory_space=pl.ANY),
        ],
        out_specs=pl.BlockSpec(memory_space=pl.ANY),
        out_shape=jax.ShapeDtypeStruct(xs.shape, xs.dtype),
        scratch_shapes=[pltpu.VMEM((2, tm, width), x1p.dtype), pltpu.SemaphoreType.DMA((2,))],
        input_output_aliases={2: 0},
        name="moe_dispatch",
        compiler_params=_cparams(("arbitrary",)),
    )(d3, x1p, xs)


def _expert_kernel(te_ref, nu_ref, xs_ref, wgu_ref, bgu_ref, wd_ref, bd_ref, o_ref, xb_ref, *, fc):
    i = pl.program_id(0)
    n_used = nu_ref[0]
    half = D_MODEL // 2

    @pl.when(i < n_used)
    def _():
        lo, hi = _unpack_bf16_pairs(xs_ref[...])
        xb_ref[:, :half] = lo
        xb_ref[:, half:] = hi
        xb = xb_ref[...]
        acc = jnp.broadcast_to(bd_ref[...], (xb.shape[0], D_MODEL))
        for c in range(D_EXPERT // fc):
            cg = slice(c * fc, (c + 1) * fc)
            cu = slice(D_EXPERT + c * fc, D_EXPERT + (c + 1) * fc)
            hg = jnp.dot(xb, wgu_ref[:, cg], preferred_element_type=F32) + bgu_ref[:, cg]
            hu = jnp.dot(xb, wgu_ref[:, cu], preferred_element_type=F32) + bgu_ref[:, cu]
            g = jnp.minimum(hg, SWIGLU_LIMIT)
            u = jnp.clip(hu, -SWIGLU_LIMIT, SWIGLU_LIMIT)
            act = (u + 1.0) * (g * jax.nn.sigmoid(SWIGLU_ALPHA * g))
            acc = acc + jnp.dot(act.astype(BF16), wd_ref[cg, :], preferred_element_type=F32)
        o_ref[...] = _pack_bf16_pairs(acc)

    @pl.when(i >= n_used)
    def _():
        o_ref[...] = jnp.zeros(o_ref.shape, o_ref.dtype)


def _expert_call(tile_e, n_used, xs, wgu, bgu, wd, bd):
    n_slots = xs.shape[0]
    tm = TM_MOE
    n_tiles = n_slots // tm

    def live(i, nu):
        return jnp.minimum(i, nu[0] - 1)

    once = pl.Buffered(1)
    grid_spec = pltpu.PrefetchScalarGridSpec(
        num_scalar_prefetch=2,
        grid=(n_tiles,),
        in_specs=[
            pl.BlockSpec((tm, D_MODEL // 2), lambda i, te, nu: (live(i, nu), 0)),
            pl.BlockSpec((None, D_MODEL, 2 * D_EXPERT), lambda i, te, nu: (te[i], 0, 0), pipeline_mode=once),
            pl.BlockSpec((None, 1, 2 * D_EXPERT), lambda i, te, nu: (te[i], 0, 0)),
            pl.BlockSpec((None, D_EXPERT, D_MODEL), lambda i, te, nu: (te[i], 0, 0)),
            pl.BlockSpec((None, 1, D_MODEL), lambda i, te, nu: (te[i], 0, 0)),
        ],
        out_specs=pl.BlockSpec((tm, D_MODEL // 2), lambda i, te, nu: (i, 0)),
        scratch_shapes=[pltpu.VMEM((tm, D_MODEL), BF16)],
    )
    return pl.pallas_call(
        functools.partial(_expert_kernel, fc=FC_MOE),
        grid_spec=grid_spec,
        out_shape=jax.ShapeDtypeStruct((n_slots, D_MODEL // 2), jnp.uint32),
        name="expert_ffn",
        compiler_params=_cparams(("arbitrary",)),
    )(tile_e, n_used, xs, wgu, bgu, wd, bd)


def _row_copy(src_hbm, row, buf, slot, r, sem):
    return pltpu.make_async_copy(src_hbm.at[pl.ds(row, 1), :], buf.at[slot, pl.ds(r, 1), :], sem.at[slot])


def _start_rows(idx_ref, src_hbm, buf, slot, sem, n_rows):
    for r in range(n_rows):
        _row_copy(src_hbm, idx_ref[0, 0, r], buf, slot, r, sem).start(priority=r % 2)


def _wait_rows(src_hbm, buf, slot, sem, n_rows):
    def body(r, c):
        _row_copy(src_hbm, 0, buf, slot, r, sem).wait()
        return c
    lax.fori_loop(0, n_rows, body, 0, unroll=8)


def _combine_kernel(d0_ref, dn_ref, e_hbm, x1_ref, gate_ref, g2_ref, b2_ref, y_ref, buf, sem, *, tm, n_tiles):
    i = pl.program_id(0)
    slot = i % 2
    n_rows = TOP_K * tm

    @pl.when(i == 0)
    def _():
        _start_rows(d0_ref, e_hbm, buf, 0, sem, n_rows)

    @pl.when(i + 1 < n_tiles)
    def _():
        _start_rows(dn_ref, e_hbm, buf, 1 - slot, sem, n_rows)

    _wait_rows(e_hbm, buf, slot, sem, n_rows)
    gate = gate_ref[...]
    ffn_lo = jnp.zeros((tm, D_MODEL // 2), F32)
    ffn_hi = jnp.zeros((tm, D_MODEL // 2), F32)
    for kk in range(TOP_K):
        lo, hi = _unpack_pairs_f32(buf[slot, kk * tm:(kk + 1) * tm, :])
        ffn_lo = ffn_lo + gate[:, kk:kk + 1] * lo
        ffn_hi = ffn_hi + gate[:, kk:kk + 1] * hi
    ffn = jnp.concatenate([ffn_lo, ffn_hi], axis=1)
    y_ref[...] = _layer_norm(DEEPNORM_ALPHA * x1_ref[...] + ffn, g2_ref[...], b2_ref[...])


def _combine_call(dest, eout, x1, gate_pad, g2, b2):
    T = x1.shape[0]
    tm = TM_COMB
    n_tiles = T // tm
    d3 = dest.reshape(n_tiles, tm, TOP_K).transpose(0, 2, 1).reshape(n_tiles, 1, TOP_K * tm)
    smem_blk = lambda imap: pl.BlockSpec((1, 1, TOP_K * tm), imap, memory_space=pltpu.SMEM)
    vec = pl.BlockSpec((1, D_MODEL), lambda i: (0, 0))
    return pl.pallas_call(
        functools.partial(_combine_kernel, tm=tm, n_tiles=n_tiles),
        grid=(n_tiles,),
        in_specs=[
            smem_blk(lambda i: (0, 0, 0)),
            smem_blk(lambda i: (jnp.minimum(i + 1, n_tiles - 1), 0, 0)),
            pl.BlockSpec(memory_space=pl.ANY),
            pl.BlockSpec((tm, D_MODEL), lambda i: (i, 0)),
            pl.BlockSpec((tm, LANES), lambda i: (i, 0)),
            vec, vec,
        ],
        out_specs=pl.BlockSpec((tm, D_MODEL), lambda i: (i, 0)),
        out_shape=jax.ShapeDtypeStruct((T, D_MODEL), F32),
        scratch_shapes=[pltpu.VMEM((2, TOP_K * tm, D_MODEL // 2), jnp.uint32),
                        pltpu.SemaphoreType.DMA((2,))],
        name="combine_ln",
        compiler_params=_cparams(("arbitrary",)),
    )(d3, d3, eout, x1, gate_pad, g2, b2)


def _rope_tables(seq):
    pos = jnp.arange(seq, dtype=jnp.int32)
    row = (pos // GRID_W).astype(F32)
    col = (pos % GRID_W).astype(F32)
    inv_freq = ROPE_THETA ** (-jnp.arange(N_FREQ, dtype=F32) / N_FREQ)
    ang_r = row[:, None] * inv_freq[None, :]
    ang_c = col[:, None] * inv_freq[None, :]
    zero = jnp.zeros_like(ang_r)
    cos = jnp.concatenate([jnp.cos(ang_r)] * 2 + [jnp.cos(ang_c)] * 2, axis=-1)
    sa = jnp.concatenate([-jnp.sin(ang_r), zero, -jnp.sin(ang_c), zero], axis=-1)
    sb = jnp.concatenate([zero, jnp.sin(ang_r), zero, jnp.sin(ang_c)], axis=-1)
    return cos, sa, sb


def _dispatch_plan(idx, rank, counts, tm):
    T = idx.shape[0]
    n_tiles = T * TOP_K // tm + N_EXPERTS
    padded = (counts + tm - 1) // tm * tm
    pend = jnp.cumsum(padded)
    pstart = pend - padded
    onehot = idx[:, :, None] == jnp.arange(N_EXPERTS, dtype=jnp.int32)[None, None, :]
    dest = jnp.sum(jnp.where(onehot, pstart[None, None, :], 0), axis=-1) + rank
    n_used = (pend[-1] // tm).astype(jnp.int32)
    tile_start = jnp.minimum(jnp.arange(n_tiles, dtype=jnp.int32), n_used - 1) * tm
    tile_e = jnp.minimum(jnp.sum((tile_start[:, None] >= pend[None, :]).astype(jnp.int32), axis=1), N_EXPERTS - 1)
    return tile_e, n_used.reshape(1), dest, n_tiles * tm


def _mixers(x3, p, counts):
    batch, seq, _ = x3.shape
    x = x3.reshape(batch * seq, D_MODEL)
    cos, sa, sb = _rope_tables(seq)
    q, k, v, gates, xn = _qkvg_call(x, p["ln_in_g"], p["ln_in_b"], p["w_in"], cos, sa, sb,
                                    p["q_norm"], p["k_norm"], seq)
    conv_in = _conv_call(xn, p["w_in"], p["conv_w"], seq)
    attn = _attention(q, k, v, p["q_norm"], p["k_norm"], batch, seq)
    return _mix_out_call(x, p["ln_in_g"], p["ln_in_b"], attn, conv_in, gates,
                         p["w_attn_out"], p["w_conv_out"], p["w_o"],
                         p["ln1_g"], p["ln1_b"], p["w_router_hi"], p["w_router_lo"], p["b_router"], counts)


def _trunks(xs3, p):
    counts = jnp.zeros((1, N_EXPERTS), F32)
    mixed = []
    for x3 in xs3:
        mixed.append(_mixers(x3, p, counts))
        counts = mixed[-1][5]
    idx_all = jnp.concatenate([m[2][:, :TOP_K] for m in mixed], axis=0)
    rank_all = jnp.concatenate([m[4][:, :TOP_K] for m in mixed], axis=0)
    tile_e, n_used, dest_all, n_slots = _dispatch_plan(idx_all, rank_all, counts[0].astype(jnp.int32), TM_MOE)
    slots = jnp.zeros((n_slots, D_MODEL // 2), jnp.uint32)
    row0 = 0
    dests = []
    for m in mixed:
        n_tok = m[0].shape[0]
        dests.append(dest_all[row0:row0 + n_tok])
        slots = _dispatch_call(dests[-1], m[1], slots)
        row0 += n_tok
    eout = _expert_call(tile_e, n_used, slots, p["w_gate_up"], p["b_gate_up"], p["w_down"], p["b_down"])
    outs = []
    for x3, m, dest in zip(xs3, mixed, dests):
        y = _combine_call(dest, eout, m[0], m[3], p["ln2_g"], p["ln2_b"])
        outs.append(y.reshape(x3.shape))
    return tuple(outs)


def _prepare(ln_in_g, ln_in_b, w_in, q_norm, k_norm, conv_w, w_attn_out, w_conv_out, w_o,
             ln1_g, ln1_b, w_router, b_router, w_gate_up, b_gate_up, w_down, b_down, ln2_g, ln2_b):
    assert w_in.shape[0] == DEPTH
    vec = lambda a: a.reshape(1, -1)
    return {
        "ln_in_g": vec(ln_in_g), "ln_in_b": vec(ln_in_b),
        "w_in": w_in[0].astype(BF16),
        "q_norm": vec(q_norm[0]), "k_norm": vec(k_norm[0]),
        "conv_w": conv_w[0],
        "w_attn_out": w_attn_out[0].astype(BF16),
        "w_conv_out": w_conv_out[0].astype(BF16),
        "w_o": w_o[0].astype(BF16),
        "ln1_g": vec(ln1_g[0]), "ln1_b": vec(ln1_b[0]),
        "w_router_hi": w_router[0].astype(BF16),
        "w_router_lo": (w_router[0] - w_router[0].astype(BF16).astype(F32)).astype(BF16),
        "b_router": vec(b_router[0]),
        "w_gate_up": w_gate_up[0].astype(BF16),
        "b_gate_up": b_gate_up[0].reshape(N_EXPERTS, 1, 2 * D_EXPERT),
        "w_down": w_down[0].astype(BF16),
        "b_down": b_down[0].reshape(N_EXPERTS, 1, D_MODEL),
        "ln2_g": vec(ln2_g[0]), "ln2_b": vec(ln2_b[0]),
    }


def kernel(x_prompt, x_sample, ln_in_g, ln_in_b, w_in, q_norm, k_norm, conv_w, w_attn_out, w_conv_out, w_o,
           ln1_g, ln1_b, w_router, b_router, w_gate_up, b_gate_up, w_down, b_down, ln2_g, ln2_b):
    p = _prepare(ln_in_g, ln_in_b, w_in, q_norm, k_norm, conv_w, w_attn_out, w_conv_out, w_o,
                 ln1_g, ln1_b, w_router, b_router, w_gate_up, b_gate_up, w_down, b_down, ln2_g, ln2_b)
    return _trunks((x_prompt, x_sample), p)
```
